```python
import jax, jax.numpy as jnp
from jax import lax
import numpy as np

D_MODEL = 1024
BATCH = 8
SEQ = 4096
DEPTH = 2

CHUNK = 64
Q_BLOCK = 2 * CHUNK
N_META = 16
N_MIXERS = 2
POOL_WINDOWS = (2, 4, 8, 16)
N_POOL_GROUPS = len(POOL_WINDOWS)
POOL_GROUP = D_MODEL // N_POOL_GROUPS
MAX_WINDOW = max(POOL_WINDOWS)
N_HEADS = 16
HEAD_DIM = D_MODEL // N_HEADS
D_FF = -(-8 * D_MODEL // (3 * 256)) * 256
N_POOL_LAYERS = (DEPTH + 1) // 2
N_FOX_LAYERS = DEPTH // 2
DN_ALPHA = (2.0 * DEPTH) ** 0.25
DN_BETA = (8.0 * DEPTH) ** -0.25
LN_EPS = 1e-5

kernel_name = "hybrid_pool_fox_deepnorm_meta"


def layer_norm(x, g, b):
    xf = x.astype(jnp.float32)
    mu = jnp.mean(xf, axis=-1, keepdims=True)
    var = jnp.mean(jnp.square(xf - mu), axis=-1, keepdims=True)
    y = (xf - mu) * lax.rsqrt(var + LN_EPS)
    return (y * g.astype(jnp.float32) + b.astype(jnp.float32)).astype(x.dtype)


def pool_mixer(x, w_grp, scale):
    B, L, D = x.shape
    xf = x.astype(jnp.float32)
    P = jnp.pad(jnp.cumsum(xf, axis=1), ((0, 0), (MAX_WINDOW, 0), (0, 0)))
    t1 = jnp.arange(1, L + 1, dtype=jnp.float32)
    groups = []
    for g, w in enumerate(POOL_WINDOWS):
        sl = slice(g * POOL_GROUP, (g + 1) * POOL_GROUP)
        win_sum = P[:, MAX_WINDOW:MAX_WINDOW + L, sl] - P[:, MAX_WINDOW - w:MAX_WINDOW - w + L, sl]
        cnt = jnp.minimum(t1, float(w))[None, :, None]
        groups.append(win_sum / cnt - xf[:, :, sl])
    y = jnp.stack(groups, axis=2).astype(x.dtype)
    y = jnp.einsum('blgc,gce->blge', y, w_grp).reshape(B, L, D)
    return y * scale


def fox_mixer(x, w_in, b_f, w_o):
    B, L, D = x.shape
    proj = jnp.einsum('bld,de->ble', x, w_in)
    q, k, v, f_logit = jnp.split(proj, [D, 2 * D, 3 * D], axis=-1)
    to_heads = lambda t: t.reshape(B, L, N_HEADS, HEAD_DIM).transpose(0, 2, 1, 3)
    q, k, v = to_heads(q), to_heads(k), to_heads(v)
    log_f = jax.nn.log_sigmoid((f_logit + b_f).astype(jnp.float32))
    c = jnp.cumsum(log_f, axis=1).transpose(0, 2, 1)
    Lp = -(-L // Q_BLOCK) * Q_BLOCK
    pad = Lp - L
    padl = lambda t: jnp.pad(t, ((0, 0), (0, 0), (0, pad), (0, 0)))
    q, k, v = padl(q), padl(k), padl(v)
    c = jnp.pad(c, ((0, 0), (0, 0), (0, pad)))
    scale = HEAD_DIM ** -0.5
    neg = jnp.finfo(jnp.float32).min
    outs = []
    for qb in range(Lp // Q_BLOCK):
        q0, q1 = qb * Q_BLOCK, (qb + 1) * Q_BLOCK
        s = jnp.einsum('bhqd,bhkd->bhqk', q[:, :, q0:q1], k[:, :, :q1],
                       preferred_element_type=jnp.float32) * scale
        s = s + c[:, :, q0:q1, None] - c[:, :, None, :q1]
        mask = jnp.arange(q0, q1)[:, None] >= jnp.arange(q1)[None, :]
        s = jnp.where(mask[None, None], s, neg)
        p = jax.nn.softmax(s, axis=-1).astype(v.dtype)
        outs.append(jnp.einsum('bhqk,bhkd->bhqd', p, v[:, :, :q1]))
    o = jnp.concatenate(outs, axis=2)[:, :, :L]
    o = o.transpose(0, 2, 1, 3).reshape(B, L, D)
    return jnp.einsum('bld,de->ble', o, w_o)


def swiglu(x, w_gate, w_up, w_down):
    hg = jnp.einsum('bld,df->blf', x, w_gate)
    hu = jnp.einsum('bld,df->blf', x, w_up)
    return jnp.einsum('blf,fd->bld', jax.nn.silu(hg) * hu, w_down)


def setup_inputs(seed: int = 0) -> dict:
    key = jax.random.key(seed)
    ks = jax.random.split(key, 16)
    nrm = lambda k, shp: jax.random.normal(k, shp, dtype=jnp.float32)
    D, H, C, F = D_MODEL, N_HEADS, POOL_GROUP, D_FF
    x = nrm(ks[0], (BATCH, SEQ, D))
    meta_tokens = nrm(ks[1], (N_META, D))
    pool_w = nrm(ks[2], (N_POOL_LAYERS, N_POOL_GROUPS, C, C)) * (C ** -0.5) * DN_BETA
    pool_scale = 1.0 + 0.02 * nrm(ks[3], (N_POOL_LAYERS, D))
    w_qk = nrm(ks[4], (N_FOX_LAYERS, D, 2 * D)) * (D ** -0.5)
    w_v = nrm(ks[5], (N_FOX_LAYERS, D, D)) * (D ** -0.5) * DN_BETA
    w_f = nrm(ks[6], (N_FOX_LAYERS, D, H)) * (D ** -0.5)
    fox_w_in = jnp.concatenate([w_qk, w_v, w_f], axis=-1)
    fox_b_f = 2.0 + 0.1 * nrm(ks[7], (N_FOX_LAYERS, H))
    fox_w_o = nrm(ks[8], (N_FOX_LAYERS, D, D)) * (D ** -0.5) * DN_BETA
    ffn_w_gate = nrm(ks[9], (DEPTH, D, F)) * (D ** -0.5) * DN_BETA
    ffn_w_up = nrm(ks[10], (DEPTH, D, F)) * (D ** -0.5) * DN_BETA
    ffn_w_down = nrm(ks[11], (DEPTH, F, D)) * (F ** -0.5) * DN_BETA
    ln_g = 1.0 + 0.02 * nrm(ks[12], (DEPTH, 2, D))
    ln_b = 0.02 * nrm(ks[13], (DEPTH, 2, D))
    return {"x": x, "meta_tokens": meta_tokens, "pool_w": pool_w, "pool_scale": pool_scale,
            "fox_w_in": fox_w_in, "fox_b_f": fox_b_f, "fox_w_o": fox_w_o,
            "ffn_w_gate": ffn_w_gate, "ffn_w_up": ffn_w_up, "ffn_w_down": ffn_w_down,
            "ln_g": ln_g, "ln_b": ln_b}


def reference(x, meta_tokens, pool_w, pool_scale, fox_w_in, fox_b_f, fox_w_o,
              ffn_w_gate, ffn_w_up, ffn_w_down, ln_g, ln_b):
    B = x.shape[0]
    meta = jnp.broadcast_to(meta_tokens[None].astype(x.dtype), (B, N_META, D_MODEL))
    h = jnp.concatenate([meta, x], axis=1)
    for i in range(DEPTH):
        j = i // N_MIXERS
        if i % N_MIXERS == 0:
            m = pool_mixer(h, pool_w[j], pool_scale[j])
        else:
            m = fox_mixer(h, fox_w_in[j], fox_b_f[j], fox_w_o[j])
        h = layer_norm(DN_ALPHA * h + m, ln_g[i, 0], ln_b[i, 0])
        f = swiglu(h, ffn_w_gate[i], ffn_w_up[i], ffn_w_down[i])
        h = layer_norm(DN_ALPHA * h + f, ln_g[i, 1], ln_b[i, 1])
    return h[:, N_META:]
```

```python
import functools

import jax
import jax.numpy as jnp
from jax import lax
from jax.experimental import pallas as pl
from jax.experimental.pallas import tpu as pltpu

N_META = 16
N_HEADS = 16
HEAD_DIM = 64
POOL_WINDOWS = (2, 4, 8, 16)
MAX_WINDOW = max(POOL_WINDOWS)
DEPTH = 2
DN_ALPHA = (2.0 * DEPTH) ** 0.25
LN_EPS = 1e-5
LANES = 128
META_ROWS = 128
MASK_VALUE = -1e30

F32 = jnp.float32
BF16 = jnp.bfloat16


def _layer_norm(z, g, b):
    mu = jnp.mean(z, axis=-1, keepdims=True)
    zc = z - mu
    var = jnp.mean(zc * zc, axis=-1, keepdims=True)
    return zc * lax.rsqrt(var + LN_EPS) * g + b


def _const_spec(shape, single_buffer=False):
    zeros = (0,) * len(shape)
    if single_buffer:
        return pl.BlockSpec(shape, lambda *_: zeros, pipeline_mode=pl.Buffered(1))
    return pl.BlockSpec(shape, lambda *_: zeros)


def _pool_ln_kernel(x_ref, halo_ref, meta_ref, w_ref, sc_ref, g_ref, b_ref, o_ref, ext_ref,
                    *, tm, is_meta):
    d = x_ref.shape[-1]
    cgrp = d // len(POOL_WINDOWS)
    cur = x_ref[...]
    ext_ref[MAX_WINDOW:, :] = cur
    if is_meta:
        ext_ref[:MAX_WINDOW, :] = jnp.zeros((MAX_WINDOW, d), F32)
        pos0 = 0
    else:
        j = pl.program_id(1)

        @pl.when(j == 0)
        def _():
            ext_ref[:MAX_WINDOW, :] = meta_ref[...]

        @pl.when(j > 0)
        def _():
            ext_ref[:MAX_WINDOW, :] = halo_ref[...]

        pos0 = N_META + j * tm
    t1 = (pos0 + 1 + lax.broadcasted_iota(jnp.int32, (tm, 1), 0)).astype(F32)
    zs = []
    for g, w in enumerate(POOL_WINDOWS):
        sl = slice(g * cgrp, (g + 1) * cgrp)
        cur_g = cur[:, sl]
        win = cur_g
        for i in range(1, w):
            win = win + ext_ref[pl.ds(MAX_WINDOW - i, tm), sl]
        cnt = jnp.minimum(t1, float(w))
        y = win / cnt - cur_g
        m = jnp.dot(y.astype(BF16), w_ref[g], preferred_element_type=F32) * sc_ref[:, sl]
        zs.append(DN_ALPHA * cur_g + m)
    z = jnp.concatenate(zs, axis=1)
    o_ref[...] = _layer_norm(z, g_ref[...], b_ref[...])


def _pool_ln(x, meta_tokens, meta_pad, pw, psc, g, b, *, tm):
    bsz, s, d = x.shape
    ngrp = len(POOL_WINDOWS)
    cgrp = d // ngrp
    hb = tm // MAX_WINDOW
    consts = [_const_spec((ngrp, cgrp, cgrp)), _const_spec((1, d)), _const_spec((1, d)), _const_spec((1, d))]
    hx = pl.pallas_call(
        functools.partial(_pool_ln_kernel, tm=tm, is_meta=False),
        grid=(bsz, s // tm),
        in_specs=[pl.BlockSpec((None, tm, d), lambda bi, j: (bi, j, 0)),
                  pl.BlockSpec((None, MAX_WINDOW, d), lambda bi, j: (bi, jnp.maximum(j * hb - 1, 0), 0)),
                  _const_spec((N_META, d))] + consts,
        out_specs=pl.BlockSpec((None, tm, d), lambda bi, j: (bi, j, 0)),
        out_shape=jax.ShapeDtypeStruct((bsz, s, d), F32),
        scratch_shapes=[pltpu.VMEM((MAX_WINDOW + tm, d), F32)],
        compiler_params=pltpu.CompilerParams(dimension_semantics=("arbitrary", "arbitrary")),
        name="pool_ln_x",
    )(x, x, meta_tokens, pw, psc, g, b)
    hm = pl.pallas_call(
        functools.partial(_pool_ln_kernel, tm=META_ROWS, is_meta=True),
        grid=(1,),
        in_specs=[_const_spec((META_ROWS, d)), _const_spec((MAX_WINDOW, d)), _const_spec((N_META, d))] + consts,
        out_specs=_const_spec((META_ROWS, d)),
        out_shape=jax.ShapeDtypeStruct((META_ROWS, d), F32),
        scratch_shapes=[pltpu.VMEM((MAX_WINDOW + META_ROWS, d), F32)],
        name="pool_ln_meta",
    )(meta_pad, meta_tokens, meta_tokens, pw, psc, g, b)
    return hx, hm


def _swiglu(hb, wg_ref, wu_ref, wd_ref, fc):
    f_dim = wg_ref.shape[1]
    out = None
    for c in range(f_dim // fc):
        sl = slice(c * fc, (c + 1) * fc)
        hg = jnp.dot(hb, wg_ref[:, sl], preferred_element_type=F32)
        hu = jnp.dot(hb, wu_ref[:, sl], preferred_element_type=F32)
        a = (jax.nn.silu(hg) * hu).astype(BF16)
        part = jnp.dot(a, wd_ref[sl, :], preferred_element_type=F32)
        out = part if out is None else out + part
    return out


def _ffn_ln_kernel(h_ref, wg_ref, wu_ref, wd_ref, g_ref, b_ref, o_ref, *, fc):
    h = h_ref[...]
    f = _swiglu(h.astype(BF16), wg_ref, wu_ref, wd_ref, fc)
    o_ref[...] = _layer_norm(DN_ALPHA * h + f, g_ref[...], b_ref[...])


def _proj_ffn_ln_kernel(a_ref, h_ref, wo_ref, g1_ref, b1_ref, wg_ref, wu_ref, wd_ref, g2_ref, b2_ref,
                        o_ref, *, fc):
    m = jnp.dot(a_ref[...], wo_ref[...], preferred_element_type=F32)
    h = _layer_norm(DN_ALPHA * h_ref[...] + m, g1_ref[...], b1_ref[...])
    f = _swiglu(h.astype(BF16), wg_ref, wu_ref, wd_ref, fc)
    o_ref[...] = _layer_norm(DN_ALPHA * h + f, g2_ref[...], b2_ref[...])


def _ffn_vmem_limit(tm, d, f, fc, with_proj):
    weights = 3 * d * f * 2 + (d * d * 2 if with_proj else 0)
    tiles = 2 * 2 * tm * d * 4 + (2 * tm * d * 2 + 2 * tm * d * 4 if with_proj else 0)
    temps = tm * fc * (4 + 4 + 4 + 2) + 4 * tm * d * 4
    return weights + tiles + temps + (8 << 20)


def _ffn_ln(h, wg, wu, wd, g, b, *, tm, fc):
    n, d = h.shape
    f = wg.shape[1]
    row = pl.BlockSpec((tm, d), lambda i: (i, 0))
    return pl.pallas_call(
        functools.partial(_ffn_ln_kernel, fc=fc),
        grid=(n // tm,),
        in_specs=[row, _const_spec((d, f), True), _const_spec((d, f), True), _const_spec((f, d), True),
                  _const_spec((1, d)), _const_spec((1, d))],
        out_specs=row,
        out_shape=jax.ShapeDtypeStruct((n, d), F32),
        compiler_params=pltpu.CompilerParams(dimension_semantics=("arbitrary",),
                                             vmem_limit_bytes=_ffn_vmem_limit(tm, d, f, fc, False)),
        name=f"ffn_ln_{n}",
    )(h, wg, wu, wd, g, b)


def _proj_ffn_ln(a, h, wo, g1, b1, wg, wu, wd, g2, b2, *, tm, fc):
    n, d = h.shape
    f = wg.shape[1]
    row = pl.BlockSpec((tm, d), lambda i: (i, 0))
    return pl.pallas_call(
        functools.partial(_proj_ffn_ln_kernel, fc=fc),
        grid=(n // tm,),
        in_specs=[row, row, _const_spec((d, d), True), _const_spec((1, d)), _const_spec((1, d)),
                  _const_spec((d, f), True), _const_spec((d, f), True), _const_spec((f, d), True),
                  _const_spec((1, d)), _const_spec((1, d))],
        out_specs=row,
        out_shape=jax.ShapeDtypeStruct((n, d), F32),
        compiler_params=pltpu.CompilerParams(dimension_semantics=("arbitrary",),
                                             vmem_limit_bytes=_ffn_vmem_limit(tm, d, f, fc, True)),
        name="proj_ffn_ln",
    )(a, h, wo, g1, b1, wg, wu, wd, g2, b2)


def _inproj_kernel(h_ref, w_ref, bf_ref, c0_ref, qe_ref, qo_ref, k_ref, v_ref, c_ref, carry_ref, *, tm):
    d = h_ref.shape[-1]

    @pl.when(pl.program_id(1) == 0)
    def _():
        carry_ref[...] = c0_ref[...]

    proj = jnp.dot(h_ref[...].astype(BF16), w_ref[...], preferred_element_type=F32)
    q = proj[:, :d] * (HEAD_DIM ** -0.5)
    first = lax.broadcasted_iota(jnp.int32, (tm, d), 1) % LANES < HEAD_DIM
    qe_ref[...] = jnp.where(first, q, 0.0).astype(BF16)
    qo_ref[...] = jnp.where(first, 0.0, q).astype(BF16)
    k_ref[...] = proj[:, d:2 * d].astype(BF16)
    v_ref[...] = proj[:, 2 * d:3 * d].astype(BF16)
    z = proj[:, 3 * d:3 * d + LANES] + bf_ref[...]
    logf = jnp.minimum(z, 0.0) - jnp.log1p(jnp.exp(-jnp.abs(z)))
    hi = logf.astype(BF16)
    r1 = logf - hi.astype(F32)
    mid = r1.astype(BF16)
    lo = (r1 - mid.astype(F32)).astype(BF16)
    tri = (lax.broadcasted_iota(jnp.int32, (tm, tm), 0)
           >= lax.broadcasted_iota(jnp.int32, (tm, tm), 1)).astype(BF16)
    cum = (jnp.dot(tri, hi, preferred_element_type=F32)
           + jnp.dot(tri, mid, preferred_element_type=F32)
           + jnp.dot(tri, lo, preferred_element_type=F32))
    c = carry_ref[...] + cum
    c_ref[...] = c
    carry_ref[...] = c[tm - 1:tm, :]


def _inproj(h, w_all, bfp, c0, *, tm):
    bsz, s, d = h.shape
    n_out = w_all.shape[1]
    row = pl.BlockSpec((None, tm, d), lambda bi, j: (bi, j, 0))
    crow = pl.BlockSpec((None, tm, LANES), lambda bi, j: (bi, j, 0))
    act = jax.ShapeDtypeStruct((bsz, s, d), BF16)
    return pl.pallas_call(
        functools.partial(_inproj_kernel, tm=tm),
        grid=(bsz, s // tm),
        in_specs=[row, _const_spec((d, n_out), True), _const_spec((1, LANES)), _const_spec((1, LANES))],
        out_specs=[row, row, row, row, crow],
        out_shape=[act, act, act, act, jax.ShapeDtypeStruct((bsz, s, LANES), F32)],
        scratch_shapes=[pltpu.VMEM((1, LANES), F32)],
        compiler_params=pltpu.CompilerParams(dimension_semantics=("arbitrary", "arbitrary"),
                                             vmem_limit_bytes=48 << 20),
        name=f"inproj_{bsz * s}",
    )(h, w_all, bfp, c0)


def _attn_kernel(qe_ref, qo_ref, k_ref, v_ref, km_ref, vm_ref, crow_ref, cmrow_ref, ctok_ref, o_ref,
                 m_sc, l_sc, acc_sc, *, tq):
    hp = pl.program_id(1)
    j = pl.program_id(2)
    lane = lax.broadcasted_iota(jnp.int32, (tq, LANES), 1)
    causal = (lax.broadcasted_iota(jnp.int32, (tq, tq), 0)
              >= lax.broadcasted_iota(jnp.int32, (tq, tq), 1))
    ctok = ctok_ref[...]
    outs = []
    for hh, q_ref in enumerate((qe_ref, qo_ref)):
        h = 2 * hp + hh
        q = q_ref[...]
        ct = jnp.sum(jnp.where(lane == h, ctok, 0.0), axis=1, keepdims=True)

        def step(kblk, vblk, cs, mask, first, hh=hh, q=q, ct=ct):
            s = lax.dot_general(q, kblk, (((1,), (1,)), ((), ())), preferred_element_type=F32)
            u = s - cs
            if mask is not None:
                u = jnp.where(mask, u, MASK_VALUE)
            m_blk = jnp.max(u, axis=1, keepdims=True) + ct
            if first:
                m_new = m_blk
            else:
                m_old = m_sc[hh]
                m_new = jnp.maximum(m_old, m_blk)
            p = jnp.exp(u - (m_new - ct))
            psum = jnp.sum(p, axis=1, keepdims=True)
            pv = jnp.dot(p.astype(BF16), vblk, preferred_element_type=F32)
            if first:
                l_sc[hh] = psum
                acc_sc[hh] = pv
            else:
                alpha = jnp.exp(m_old - m_new)
                l_sc[hh] = alpha * l_sc[hh] + psum
                acc_sc[hh] = alpha * acc_sc[hh] + pv
            m_sc[hh] = m_new

        step(km_ref[:N_META, :], vm_ref[:N_META, :], cmrow_ref[pl.ds(h, 1), :], None, True)

        def body(kc, carry, h=h, step=step):
            k0 = pl.multiple_of(kc * tq, tq)
            step(k_ref[pl.ds(k0, tq), :], v_ref[pl.ds(k0, tq), :],
                 crow_ref[pl.ds(h, 1), pl.ds(k0, tq)], None, False)
            return carry

        lax.fori_loop(0, j, body, 0)
        k0 = pl.multiple_of(j * tq, tq)
        step(k_ref[pl.ds(k0, tq), :], v_ref[pl.ds(k0, tq), :],
             crow_ref[pl.ds(h, 1), pl.ds(k0, tq)], causal, False)
        outs.append(acc_sc[hh] / l_sc[hh])
    o_ref[...] = jnp.where(lane < HEAD_DIM, outs[0], outs[1]).astype(BF16)


def _attention(qe, qo, k, v, km, vm, crow, cmrow, ctok, *, tq):
    bsz, s, d = k.shape
    qspec = pl.BlockSpec((None, tq, LANES), lambda bi, hp, j: (bi, j, hp))
    kvspec = pl.BlockSpec((None, s, LANES), lambda bi, hp, j: (bi, 0, hp))
    mspec = pl.BlockSpec((META_ROWS, LANES), lambda bi, hp, j: (0, hp))
    return pl.pallas_call(
        functools.partial(_attn_kernel, tq=tq),
        grid=(bsz, d // LANES, s // tq),
        in_specs=[qspec, qspec, kvspec, kvspec, mspec, mspec,
                  pl.BlockSpec((None, N_HEADS, s), lambda bi, hp, j: (bi, 0, 0)),
                  _const_spec((N_HEADS, N_META)),
                  pl.BlockSpec((None, tq, LANES), lambda bi, hp, j: (bi, j, 0))],
        out_specs=qspec,
        out_shape=jax.ShapeDtypeStruct((bsz, s, d), BF16),
        scratch_shapes=[pltpu.VMEM((2, tq, 1), F32), pltpu.VMEM((2, tq, 1), F32),
                        pltpu.VMEM((2, tq, LANES), F32)],
        compiler_params=pltpu.CompilerParams(dimension_semantics=("arbitrary", "arbitrary", "arbitrary")),
        name="fox_attn",
    )(qe, qo, k, v, km, vm, crow, cmrow, ctok)


def kernel(x, meta_tokens, pool_w, pool_scale, fox_w_in, fox_b_f, fox_w_o, ffn_w_gate, ffn_w_up,
           ffn_w_down, ln_g, ln_b):
    bsz, s, d = x.shape
    f_dim = ffn_w_gate.shape[-1]
    tm = 512
    fc = f_dim // 2
    row = lambda a: a.reshape(1, -1).astype(F32)

    meta_pad = jnp.pad(meta_tokens, ((0, META_ROWS - N_META), (0, 0)))
    wg = ffn_w_gate.astype(BF16)
    wu = ffn_w_up.astype(BF16)
    wd = ffn_w_down.astype(BF16)

    h1x, h1m = _pool_ln(x, meta_tokens, meta_pad, pool_w[0].astype(BF16), row(pool_scale[0]),
                        row(ln_g[0, 0]), row(ln_b[0, 0]), tm=tm)
    ffn0 = functools.partial(_ffn_ln, wg=wg[0], wu=wu[0], wd=wd[0], g=row(ln_g[0, 1]), b=row(ln_b[0, 1]), fc=fc)
    h2x = ffn0(h1x.reshape(bsz * s, d), tm=tm)
    h2m = ffn0(h1m, tm=META_ROWS)

    w_all = jnp.pad(fox_w_in[0], ((0, 0), (0, LANES - N_HEADS))).astype(BF16)
    bfp = jnp.pad(fox_b_f[0], (0, LANES - N_HEADS)).reshape(1, LANES)
    _, _, km, vm, cm = _inproj(h2m[None], w_all, bfp, jnp.zeros((1, LANES), F32), tm=META_ROWS)
    qe, qo, k, v, ctok = _inproj(h2x.reshape(bsz, s, d), w_all, bfp, cm[0, N_META - 1:N_META, :], tm=tm)
    crow = ctok[:, :, :N_HEADS].transpose(0, 2, 1)
    cmrow = cm[0, :N_META, :N_HEADS].T
    o = _attention(qe, qo, k, v, km[0], vm[0], crow, cmrow, ctok, tq=256)
    out = _proj_ffn_ln(o.reshape(bsz * s, d), h2x, fox_w_o[0].astype(BF16), row(ln_g[1, 0]), row(ln_b[1, 0]),
                       wg[1], wu[1], wd[1], row(ln_g[1, 1]), row(ln_b[1, 1]), tm=tm, fc=fc)
    return out.reshape(bsz, s, d)
```

```python
import functools
import math

import jax
import jax.numpy as jnp
from jax import lax
from jax.experimental import pallas as pl
from jax.experimental.pallas import tpu as pltpu

N_META = 16
N_HEADS = 16
HEAD_DIM = 64
POOL_WINDOWS = (2, 4, 8, 16)
MAX_WINDOW = max(POOL_WINDOWS)
DEPTH = 2
DN_ALPHA = (2.0 * DEPTH) ** 0.25
LN_EPS = 1e-5
LANES = 128
META_ROWS = 128
MASK_VALUE = -1e30
LOG2E = math.log2(math.e)

F32 = jnp.float32
BF16 = jnp.bfloat16


def _layer_norm(z, g, b):
    mu = jnp.mean(z, axis=-1, keepdims=True)
    zc = z - mu
    var = jnp.mean(zc * zc, axis=-1, keepdims=True)
    return zc * lax.rsqrt(var + LN_EPS) * g + b


def _const_spec(shape, single_buffer=False):
    zeros = (0,) * len(shape)
    if single_buffer:
        return pl.BlockSpec(shape, lambda *_: zeros, pipeline_mode=pl.Buffered(1))
    return pl.BlockSpec(shape, lambda *_: zeros)


def _pool_ln_kernel(x_ref, halo_ref, meta_ref, w_ref, sc_ref, g_ref, b_ref, o_ref, ext_ref,
                    *, tm, is_meta):
    d = x_ref.shape[-1]
    cgrp = d // len(POOL_WINDOWS)
    cur = x_ref[...]
    ext_ref[MAX_WINDOW:, :] = cur
    if is_meta:
        ext_ref[:MAX_WINDOW, :] = jnp.zeros((MAX_WINDOW, d), F32)
        pos0 = 0
    else:
        j = pl.program_id(1)

        @pl.when(j == 0)
        def _():
            ext_ref[:MAX_WINDOW, :] = meta_ref[...]

        @pl.when(j > 0)
        def _():
            ext_ref[:MAX_WINDOW, :] = halo_ref[...]

        pos0 = N_META + j * tm
    t1 = (pos0 + 1 + lax.broadcasted_iota(jnp.int32, (tm, 1), 0)).astype(F32)
    zs = []
    for g, w in enumerate(POOL_WINDOWS):
        sl = slice(g * cgrp, (g + 1) * cgrp)
        cur_g = cur[:, sl]
        win = cur_g
        for i in range(1, w):
            win = win + ext_ref[pl.ds(MAX_WINDOW - i, tm), sl]
        cnt = jnp.minimum(t1, float(w))
        y = win / cnt - cur_g
        m = jnp.dot(y.astype(BF16), w_ref[g], preferred_element_type=F32) * sc_ref[:, sl]
        zs.append(DN_ALPHA * cur_g + m)
    z = jnp.concatenate(zs, axis=1)
    o_ref[...] = _layer_norm(z, g_ref[...], b_ref[...])


def _pool_ln(x, meta_tokens, meta_pad, pw, psc, g, b, *, tm):
    bsz, s, d = x.shape
    ngrp = len(POOL_WINDOWS)
    cgrp = d // ngrp
    hb = tm // MAX_WINDOW
    consts = [_const_spec((ngrp, cgrp, cgrp)), _const_spec((1, d)), _const_spec((1, d)), _const_spec((1, d))]
    hx = pl.pallas_call(
        functools.partial(_pool_ln_kernel, tm=tm, is_meta=False),
        grid=(bsz, s // tm),
        in_specs=[pl.BlockSpec((None, tm, d), lambda bi, j: (bi, j, 0)),
                  pl.BlockSpec((None, MAX_WINDOW, d), lambda bi, j: (bi, jnp.maximum(j * hb - 1, 0), 0)),
                  _const_spec((N_META, d))] + consts,
        out_specs=pl.BlockSpec((None, tm, d), lambda bi, j: (bi, j, 0)),
        out_shape=jax.ShapeDtypeStruct((bsz, s, d), F32),
        scratch_shapes=[pltpu.VMEM((MAX_WINDOW + tm, d), F32)],
        compiler_params=pltpu.CompilerParams(dimension_semantics=("arbitrary", "arbitrary")),
        name="pool_ln_x",
    )(x, x, meta_tokens, pw, psc, g, b)
    hm = pl.pallas_call(
        functools.partial(_pool_ln_kernel, tm=META_ROWS, is_meta=True),
        grid=(1,),
        in_specs=[_const_spec((META_ROWS, d)), _const_spec((MAX_WINDOW, d)), _const_spec((N_META, d))] + consts,
        out_specs=_const_spec((META_ROWS, d)),
        out_shape=jax.ShapeDtypeStruct((META_ROWS, d), F32),
        scratch_shapes=[pltpu.VMEM((MAX_WINDOW + META_ROWS, d), F32)],
        name="pool_ln_meta",
    )(meta_pad, meta_tokens, meta_tokens, pw, psc, g, b)
    return hx, hm


def _swiglu(hb, wg_ref, wu_ref, wd_ref, fc):
    f_dim = wg_ref.shape[1]
    out = None
    for c in range(f_dim // fc):
        sl = slice(c * fc, (c + 1) * fc)
        hg = jnp.dot(hb, wg_ref[:, sl], preferred_element_type=F32)
        hu = jnp.dot(hb, wu_ref[:, sl], preferred_element_type=F32)
        a = (jax.nn.silu(hg) * hu).astype(BF16)
        part = jnp.dot(a, wd_ref[sl, :], preferred_element_type=F32)
        out = part if out is None else out + part
    return out


def _ffn_ln_kernel(h_ref, wg_ref, wu_ref, wd_ref, g_ref, b_ref, o_ref, *, fc):
    h = h_ref[...]
    f = _swiglu(h.astype(BF16), wg_ref, wu_ref, wd_ref, fc)
    o_ref[...] = _layer_norm(DN_ALPHA * h + f, g_ref[...], b_ref[...])


def _proj_ffn_ln_kernel(a_ref, h_ref, wo_ref, g1_ref, b1_ref, wg_ref, wu_ref, wd_ref, g2_ref, b2_ref,
                        o_ref, *, fc):
    m = jnp.dot(a_ref[...], wo_ref[...], preferred_element_type=F32)
    h = _layer_norm(DN_ALPHA * h_ref[...] + m, g1_ref[...], b1_ref[...])
    f = _swiglu(h.astype(BF16), wg_ref, wu_ref, wd_ref, fc)
    o_ref[...] = _layer_norm(DN_ALPHA * h + f, g2_ref[...], b2_ref[...])


def _ffn_vmem_limit(tm, d, f, fc, with_proj):
    weights = 3 * d * f * 2 + (d * d * 2 if with_proj else 0)
    tiles = 2 * 2 * tm * d * 4 + (2 * tm * d * 2 + 2 * tm * d * 4 if with_proj else 0)
    temps = tm * fc * (4 + 4 + 4 + 2) + 4 * tm * d * 4
    return weights + tiles + temps + (8 << 20)


def _ffn_ln(h, wg, wu, wd, g, b, *, tm, fc):
    n, d = h.shape
    f = wg.shape[1]
    row = pl.BlockSpec((tm, d), lambda i: (i, 0))
    return pl.pallas_call(
        functools.partial(_ffn_ln_kernel, fc=fc),
        grid=(n // tm,),
        in_specs=[row, _const_spec((d, f), True), _const_spec((d, f), True), _const_spec((f, d), True),
                  _const_spec((1, d)), _const_spec((1, d))],
        out_specs=row,
        out_shape=jax.ShapeDtypeStruct((n, d), F32),
        compiler_params=pltpu.CompilerParams(dimension_semantics=("arbitrary",),
                                             vmem_limit_bytes=_ffn_vmem_limit(tm, d, f, fc, False)),
        name=f"ffn_ln_{n}",
    )(h, wg, wu, wd, g, b)


def _proj_ffn_ln(a, h, wo, g1, b1, wg, wu, wd, g2, b2, *, tm, fc):
    n, d = h.shape
    f = wg.shape[1]
    row = pl.BlockSpec((tm, d), lambda i: (i, 0))
    return pl.pallas_call(
        functools.partial(_proj_ffn_ln_kernel, fc=fc),
        grid=(n // tm,),
        in_specs=[row, row, _const_spec((d, d), True), _const_spec((1, d)), _const_spec((1, d)),
                  _const_spec((d, f), True), _const_spec((d, f), True), _const_spec((f, d), True),
                  _const_spec((1, d)), _const_spec((1, d))],
        out_specs=row,
        out_shape=jax.ShapeDtypeStruct((n, d), F32),
        compiler_params=pltpu.CompilerParams(dimension_semantics=("arbitrary",),
                                             vmem_limit_bytes=_ffn_vmem_limit(tm, d, f, fc, True)),
        name="proj_ffn_ln",
    )(a, h, wo, g1, b1, wg, wu, wd, g2, b2)


def _inproj_kernel(h_ref, w_ref, bf_ref, c0_ref, qe_ref, qo_ref, k_ref, ve_ref, vo_ref, c_ref, carry_ref,
                   *, tm):
    d = h_ref.shape[-1]

    @pl.when(pl.program_id(1) == 0)
    def _():
        carry_ref[...] = c0_ref[...]

    proj = jnp.dot(h_ref[...].astype(BF16), w_ref[...], preferred_element_type=F32)
    q = proj[:, :d] * (HEAD_DIM ** -0.5 * LOG2E)
    slab_lane = lax.broadcasted_iota(jnp.int32, (tm, d), 1) % LANES
    first = slab_lane < HEAD_DIM
    qe_ref[...] = jnp.where(first, q, 0.0).astype(BF16)
    qo_ref[...] = jnp.where(first, 0.0, q).astype(BF16)
    k_ref[...] = proj[:, d:2 * d].astype(BF16)
    v = proj[:, 2 * d:3 * d]
    ve_ref[...] = jnp.where(first, v, jnp.where(slab_lane == HEAD_DIM, 1.0, 0.0)).astype(BF16)
    vo_ref[...] = jnp.where(first, jnp.where(slab_lane == 0, 1.0, 0.0), v).astype(BF16)
    z = proj[:, 3 * d:3 * d + LANES] + bf_ref[...]
    logf = (jnp.minimum(z, 0.0) - jnp.log1p(jnp.exp(-jnp.abs(z)))) * LOG2E
    hi = logf.astype(BF16)
    r1 = logf - hi.astype(F32)
    mid = r1.astype(BF16)
    lo = (r1 - mid.astype(F32)).astype(BF16)
    tri = (lax.broadcasted_iota(jnp.int32, (tm, tm), 0)
           >= lax.broadcasted_iota(jnp.int32, (tm, tm), 1)).astype(BF16)
    cum = (jnp.dot(tri, hi, preferred_element_type=F32)
           + jnp.dot(tri, mid, preferred_element_type=F32)
           + jnp.dot(tri, lo, preferred_element_type=F32))
    c = carry_ref[...] + cum
    c_ref[...] = c
    carry_ref[...] = c[tm - 1:tm, :]


def _inproj(h, w_all, bfp, c0, *, tm):
    bsz, s, d = h.shape
    n_out = w_all.shape[1]
    row = pl.BlockSpec((None, tm, d), lambda bi, j: (bi, j, 0))
    crow = pl.BlockSpec((None, tm, LANES), lambda bi, j: (bi, j, 0))
    act = jax.ShapeDtypeStruct((bsz, s, d), BF16)
    return pl.pallas_call(
        functools.partial(_inproj_kernel, tm=tm),
        grid=(bsz, s // tm),
        in_specs=[row, _const_spec((d, n_out), True), _const_spec((1, LANES)), _const_spec((1, LANES))],
        out_specs=[row, row, row, row, row, crow],
        out_shape=[act, act, act, act, act, jax.ShapeDtypeStruct((bsz, s, LANES), F32)],
        scratch_shapes=[pltpu.VMEM((1, LANES), F32)],
        compiler_params=pltpu.CompilerParams(dimension_semantics=("arbitrary", "arbitrary"),
                                             vmem_limit_bytes=48 << 20),
        name=f"inproj_{bsz * s}",
    )(h, w_all, bfp, c0)


def _attn_kernel(qe_ref, qo_ref, k_ref, ve_ref, vo_ref, km_ref, vem_ref, vom_ref, crow_ref, cmrow_ref,
                 ctok_ref, o_ref, m_sc, acc_sc, *, tq):
    hp = pl.program_id(1)
    j = pl.program_id(2)
    nck = tq // LANES
    lane = lax.broadcasted_iota(jnp.int32, (tq, LANES), 1)
    ctok = ctok_ref[...]
    q_refs = (qe_ref, qo_ref)
    v_refs = (ve_ref, vo_ref)
    vm_refs = (vem_ref, vom_ref)
    heads = [2 * hp, 2 * hp + 1]
    qs = [r[...] for r in q_refs]
    cts = [jnp.broadcast_to(jnp.sum(jnp.where(lane == h, ctok, 0.0), axis=1, keepdims=True), (tq, LANES))
           for h in heads]

    def scores(hh, kblk, cs):
        s = lax.dot_general(qs[hh], kblk, (((1,), (1,)), ((), ())), preferred_element_type=F32)
        u = s - cs
        return [u[:, c * LANES:(c + 1) * LANES] for c in range(u.shape[1] // LANES)]

    def probs(hh, chunks, m_old):
        mx = functools.reduce(jnp.maximum, chunks)
        m_blk = jnp.broadcast_to(jnp.max(mx, axis=1, keepdims=True), (tq, LANES)) + cts[hh]
        m_new = m_blk if m_old is None else jnp.maximum(m_old, m_blk)
        shift = m_new - cts[hh]
        return m_new, [jnp.exp2(u - shift).astype(BF16) for u in chunks]

    k0 = pl.multiple_of(j * tq, tq)
    causal = (lax.broadcasted_iota(jnp.int32, (tq, tq), 0)
              >= lax.broadcasted_iota(jnp.int32, (tq, tq), 1))
    for hh in range(2):
        h = heads[hh]
        diag = scores(hh, k_ref[pl.ds(k0, tq), :], crow_ref[pl.ds(h, 1), pl.ds(k0, tq)])
        diag = [jnp.where(causal[:, c * LANES:(c + 1) * LANES], u, MASK_VALUE) for c, u in enumerate(diag)]
        meta = scores(hh, km_ref[...], cmrow_ref[pl.ds(h, 1), :])
        meta = [jnp.where(lane < N_META, meta[0], MASK_VALUE)]
        m_new, ps = probs(hh, diag + meta, None)
        pv = (jnp.dot(jnp.concatenate(ps[:nck], axis=1), v_refs[hh][pl.ds(k0, tq), :],
                      preferred_element_type=F32)
              + jnp.dot(ps[nck], vm_refs[hh][...], preferred_element_type=F32))
        m_sc[hh] = m_new
        acc_sc[hh] = pv

    def body(kc, carry):
        kk = pl.multiple_of(kc * tq, tq)
        kblk = k_ref[pl.ds(kk, tq), :]
        for hh in range(2):
            m_old = m_sc[hh]
            chunks = scores(hh, kblk, crow_ref[pl.ds(heads[hh], 1), pl.ds(kk, tq)])
            m_new, ps = probs(hh, chunks, m_old)
            pv = jnp.dot(jnp.concatenate(ps, axis=1), v_refs[hh][pl.ds(kk, tq), :],
                         preferred_element_type=F32)
            acc_sc[hh] = jnp.exp2(m_old - m_new) * acc_sc[hh] + pv
            m_sc[hh] = m_new
        return carry

    lax.fori_loop(0, j, body, 0)
    acc_e = acc_sc[0]
    acc_o = acc_sc[1]
    o = jnp.where(lane < HEAD_DIM, acc_e / acc_e[:, HEAD_DIM:HEAD_DIM + 1], acc_o / acc_o[:, 0:1])
    o_ref[...] = o.astype(BF16)


def _attention(qe, qo, k, ve, vo, km, vem, vom, crow, cmrow, ctok, *, tq):
    bsz, s, d = k.shape
    qspec = pl.BlockSpec((None, tq, LANES), lambda bi, hp, j: (bi, j, hp))
    kvspec = pl.BlockSpec((None, s, LANES), lambda bi, hp, j: (bi, 0, hp))
    mspec = pl.BlockSpec((META_ROWS, LANES), lambda bi, hp, j: (0, hp))
    return pl.pallas_call(
        functools.partial(_attn_kernel, tq=tq),
        grid=(bsz, d // LANES, s // tq),
        in_specs=[qspec, qspec, kvspec, kvspec, kvspec, mspec, mspec, mspec,
                  pl.BlockSpec((None, N_HEADS, s), lambda bi, hp, j: (bi, 0, 0)),
                  _const_spec((N_HEADS, LANES)),
                  pl.BlockSpec((None, tq, LANES), lambda bi, hp, j: (bi, j, 0))],
        out_specs=qspec,
        out_shape=jax.ShapeDtypeStruct((bsz, s, d), BF16),
        scratch_shapes=[pltpu.VMEM((2, tq, LANES), F32), pltpu.VMEM((2, tq, LANES), F32)],
        compiler_params=pltpu.CompilerParams(dimension_semantics=("arbitrary", "arbitrary", "arbitrary"),
                                             vmem_limit_bytes=48 << 20),
        name="fox_attn",
    )(qe, qo, k, ve, vo, km, vem, vom, crow, cmrow, ctok)


def kernel(x, meta_tokens, pool_w, pool_scale, fox_w_in, fox_b_f, fox_w_o, ffn_w_gate, ffn_w_up,
           ffn_w_down, ln_g, ln_b):
    bsz, s, d = x.shape
    f_dim = ffn_w_gate.shape[-1]
    tm = 512
    tq = 512
    fc = f_dim // 2
    row = lambda a: a.reshape(1, -1).astype(F32)

    meta_pad = jnp.pad(meta_tokens, ((0, META_ROWS - N_META), (0, 0)))
    wg = ffn_w_gate.astype(BF16)
    wu = ffn_w_up.astype(BF16)
    wd = ffn_w_down.astype(BF16)

    h1x, h1m = _pool_ln(x, meta_tokens, meta_pad, pool_w[0].astype(BF16), row(pool_scale[0]),
                        row(ln_g[0, 0]), row(ln_b[0, 0]), tm=tm)
    ffn0 = functools.partial(_ffn_ln, wg=wg[0], wu=wu[0], wd=wd[0], g=row(ln_g[0, 1]), b=row(ln_b[0, 1]), fc=fc)
    h2x = ffn0(h1x.reshape(bsz * s, d), tm=tm)
    h2m = ffn0(h1m, tm=META_ROWS)

    w_all = jnp.pad(fox_w_in[0], ((0, 0), (0, LANES - N_HEADS))).astype(BF16)
    bfp = jnp.pad(fox_b_f[0], (0, LANES - N_HEADS)).reshape(1, LANES)
    _, _, km, vem, vom, cm = _inproj(h2m[None], w_all, bfp, jnp.zeros((1, LANES), F32), tm=META_ROWS)
    qe, qo, k, ve, vo, ctok = _inproj(h2x.reshape(bsz, s, d), w_all, bfp, cm[0, N_META - 1:N_META, :], tm=tm)
    crow = ctok[:, :, :N_HEADS].transpose(0, 2, 1)
    cmrow = cm[0, :, :N_HEADS].T
    o = _attention(qe, qo, k, ve, vo, km[0], vem[0], vom[0], crow, cmrow, ctok, tq=tq)
    out = _proj_ffn_ln(o.reshape(bsz * s, d), h2x, fox_w_o[0].astype(BF16), row(ln_g[1, 0]), row(ln_b[1, 0]),
                       wg[1], wu[1], wd[1], row(ln_g[1, 1]), row(ln_b[1, 1]), tm=tm, fc=fc)
    return out.reshape(bsz, s, d)
```

```python
import functools
import math

import jax
import jax.numpy as jnp
import numpy as np
from jax import lax
from jax.experimental import pallas as pl
from jax.experimental.pallas import tpu as pltpu

N_META = 16
N_HEADS = 16
HEAD_DIM = 64
POOL_WINDOWS = (2, 4, 8, 16)
MAX_WINDOW = max(POOL_WINDOWS)
DEPTH = 2
DN_ALPHA = (2.0 * DEPTH) ** 0.25
LN_EPS = 1e-5
LANES = 128
META_ROWS = 128
MASK_VALUE = -1e30
LOG2E = math.log2(math.e)

F32 = jnp.float32
BF16 = jnp.bfloat16


def _layer_norm(z, g, b):
    mu = jnp.mean(z, axis=-1, keepdims=True)
    zc = z - mu
    var = jnp.mean(zc * zc, axis=-1, keepdims=True)
    return zc * lax.rsqrt(var + LN_EPS) * g + b


def _const_spec(shape, single_buffer=False):
    zeros = (0,) * len(shape)
    if single_buffer:
        return pl.BlockSpec(shape, lambda *_: zeros, pipeline_mode=pl.Buffered(1))
    return pl.BlockSpec(shape, lambda *_: zeros)


def _pool_ln_kernel(x_ref, halo_ref, meta_ref, w_ref, sc_ref, g_ref, b_ref, o_ref, ext_ref,
                    *, tm, is_meta):
    d = x_ref.shape[-1]
    cgrp = d // len(POOL_WINDOWS)
    cur = x_ref[...]
    ext_ref[MAX_WINDOW:, :] = cur
    if is_meta:
        ext_ref[:MAX_WINDOW, :] = jnp.zeros((MAX_WINDOW, d), F32)
        pos0 = 0
    else:
        j = pl.program_id(1)

        @pl.when(j == 0)
        def _():
            ext_ref[:MAX_WINDOW, :] = meta_ref[...]

        @pl.when(j > 0)
        def _():
            ext_ref[:MAX_WINDOW, :] = halo_ref[...]

        pos0 = N_META + j * tm
    t1 = (pos0 + 1 + lax.broadcasted_iota(jnp.int32, (tm, 1), 0)).astype(F32)
    zs = []
    for g, w in enumerate(POOL_WINDOWS):
        sl = slice(g * cgrp, (g + 1) * cgrp)
        cur_g = cur[:, sl]
        win = cur_g
        for i in range(1, w):
            win = win + ext_ref[pl.ds(MAX_WINDOW - i, tm), sl]
        cnt = jnp.minimum(t1, float(w))
        y = win / cnt - cur_g
        m = jnp.dot(y.astype(BF16), w_ref[g], preferred_element_type=F32) * sc_ref[:, sl]
        zs.append(DN_ALPHA * cur_g + m)
    z = jnp.concatenate(zs, axis=1)
    o_ref[...] = _layer_norm(z, g_ref[...], b_ref[...])


def _pool_ln(x, meta_tokens, meta_pad, pw, psc, g, b, *, tm):
    bsz, s, d = x.shape
    ngrp = len(POOL_WINDOWS)
    cgrp = d // ngrp
    hb = tm // MAX_WINDOW
    consts = [_const_spec((ngrp, cgrp, cgrp)), _const_spec((1, d)), _const_spec((1, d)), _const_spec((1, d))]
    hx = pl.pallas_call(
        functools.partial(_pool_ln_kernel, tm=tm, is_meta=False),
        grid=(bsz, s // tm),
        in_specs=[pl.BlockSpec((None, tm, d), lambda bi, j: (bi, j, 0)),
                  pl.BlockSpec((None, MAX_WINDOW, d), lambda bi, j: (bi, jnp.maximum(j * hb - 1, 0), 0)),
                  _const_spec((N_META, d))] + consts,
        out_specs=pl.BlockSpec((None, tm, d), lambda bi, j: (bi, j, 0)),
        out_shape=jax.ShapeDtypeStruct((bsz, s, d), F32),
        scratch_shapes=[pltpu.VMEM((MAX_WINDOW + tm, d), F32)],
        compiler_params=pltpu.CompilerParams(dimension_semantics=("arbitrary", "arbitrary")),
        name="pool_ln_x",
    )(x, x, meta_tokens, pw, psc, g, b)
    hm = pl.pallas_call(
        functools.partial(_pool_ln_kernel, tm=META_ROWS, is_meta=True),
        grid=(1,),
        in_specs=[_const_spec((META_ROWS, d)), _const_spec((MAX_WINDOW, d)), _const_spec((N_META, d))] + consts,
        out_specs=_const_spec((META_ROWS, d)),
        out_shape=jax.ShapeDtypeStruct((META_ROWS, d), F32),
        scratch_shapes=[pltpu.VMEM((MAX_WINDOW + META_ROWS, d), F32)],
        name="pool_ln_meta",
    )(meta_pad, meta_tokens, meta_tokens, pw, psc, g, b)
    return hx, hm


def _swiglu(hb, wg_ref, wu_ref, wd_ref, fc):
    f_dim = wg_ref.shape[1]
    out = None
    for c in range(f_dim // fc):
        sl = slice(c * fc, (c + 1) * fc)
        hg = jnp.dot(hb, wg_ref[:, sl], preferred_element_type=F32)
        hu = jnp.dot(hb, wu_ref[:, sl], preferred_element_type=F32)
        a = (jax.nn.silu(hg) * hu).astype(BF16)
        part = jnp.dot(a, wd_ref[sl, :], preferred_element_type=F32)
        out = part if out is None else out + part
    return out


def _ffn_ln_kernel(h_ref, wg_ref, wu_ref, wd_ref, g_ref, b_ref, o_ref, *, fc):
    h = h_ref[...]
    f = _swiglu(h.astype(BF16), wg_ref, wu_ref, wd_ref, fc)
    o_ref[...] = _layer_norm(DN_ALPHA * h + f, g_ref[...], b_ref[...])


def _proj_ffn_ln_kernel(a_ref, h_ref, wo_ref, g1_ref, b1_ref, wg_ref, wu_ref, wd_ref, g2_ref, b2_ref,
                        o_ref, *, fc):
    m = jnp.dot(a_ref[...], wo_ref[...], preferred_element_type=F32)
    h = _layer_norm(DN_ALPHA * h_ref[...] + m, g1_ref[...], b1_ref[...])
    f = _swiglu(h.astype(BF16), wg_ref, wu_ref, wd_ref, fc)
    o_ref[...] = _layer_norm(DN_ALPHA * h + f, g2_ref[...], b2_ref[...])


def _ffn_vmem_limit(tm, d, f, fc, with_proj):
    weights = 3 * d * f * 2 + (d * d * 2 if with_proj else 0)
    tiles = 2 * 2 * tm * d * 4 + (2 * tm * d * 2 + 2 * tm * d * 4 if with_proj else 0)
    temps = tm * fc * (4 + 4 + 4 + 2) + 4 * tm * d * 4
    return weights + tiles + temps + (8 << 20)


def _ffn_ln(h, wg, wu, wd, g, b, *, tm, fc):
    n, d = h.shape
    f = wg.shape[1]
    row = pl.BlockSpec((tm, d), lambda i: (i, 0))
    return pl.pallas_call(
        functools.partial(_ffn_ln_kernel, fc=fc),
        grid=(n // tm,),
        in_specs=[row, _const_spec((d, f), True), _const_spec((d, f), True), _const_spec((f, d), True),
                  _const_spec((1, d)), _const_spec((1, d))],
        out_specs=row,
        out_shape=jax.ShapeDtypeStruct((n, d), F32),
        compiler_params=pltpu.CompilerParams(dimension_semantics=("arbitrary",),
                                             vmem_limit_bytes=_ffn_vmem_limit(tm, d, f, fc, False)),
        name=f"ffn_ln_{n}",
    )(h, wg, wu, wd, g, b)


def _proj_ffn_ln(a, h, wo, g1, b1, wg, wu, wd, g2, b2, *, tm, fc):
    n, d = h.shape
    f = wg.shape[1]
    row = pl.BlockSpec((tm, d), lambda i: (i, 0))
    return pl.pallas_call(
        functools.partial(_proj_ffn_ln_kernel, fc=fc),
        grid=(n // tm,),
        in_specs=[row, row, _const_spec((d, d), True), _const_spec((1, d)), _const_spec((1, d)),
                  _const_spec((d, f), True), _const_spec((d, f), True), _const_spec((f, d), True),
                  _const_spec((1, d)), _const_spec((1, d))],
        out_specs=row,
        out_shape=jax.ShapeDtypeStruct((n, d), F32),
        compiler_params=pltpu.CompilerParams(dimension_semantics=("arbitrary",),
                                             vmem_limit_bytes=_ffn_vmem_limit(tm, d, f, fc, True)),
        name="proj_ffn_ln",
    )(a, h, wo, g1, b1, wg, wu, wd, g2, b2)


BIAS_PIECES = 3
VT_ROWS = 80


def _bf16_split(x):
    hi = x.astype(BF16)
    r1 = x - hi.astype(F32)
    mid = r1.astype(BF16)
    lo = (r1 - mid.astype(F32)).astype(BF16)
    return hi, mid, lo


def _bias_placement(d):
    p = np.zeros((LANES, d), np.float32)
    for h in range(N_HEADS):
        base = (h // 2) * LANES + (HEAD_DIM if h % 2 == 0 else 0)
        for piece in range(BIAS_PIECES):
            p[piece * N_HEADS + h, base + piece] = 1.0
            p[piece * N_HEADS + h, base + BIAS_PIECES + piece] = 1.0
    return jnp.asarray(p, BF16)


def _inproj_kernel(h_ref, w_ref, bf_ref, c0_ref, place_ref, qe_ref, qo_ref, ke_ref, ko_ref, vt_ref, c_ref,
                   carry_ref, *, tm):
    d = h_ref.shape[-1]

    @pl.when(pl.program_id(1) == 0)
    def _():
        carry_ref[...] = c0_ref[...]

    proj = jnp.dot(h_ref[...].astype(BF16), w_ref[...], preferred_element_type=F32)
    z = proj[:, 3 * d:3 * d + LANES] + bf_ref[...]
    logf = (jnp.minimum(z, 0.0) - jnp.log1p(jnp.exp(-jnp.abs(z)))) * LOG2E
    tri = (lax.broadcasted_iota(jnp.int32, (tm, tm), 0)
           >= lax.broadcasted_iota(jnp.int32, (tm, tm), 1)).astype(BF16)
    cum = sum(jnp.dot(tri, piece, preferred_element_type=F32) for piece in _bf16_split(logf))
    c = carry_ref[...] + cum
    c_ref[...] = c
    carry_ref[...] = c[tm - 1:tm, :]

    head_lane = lax.broadcasted_iota(jnp.int32, (tm, LANES), 1) < N_HEADS
    hi, mid, lo = (jnp.where(head_lane, piece.astype(F32), 0.0) for piece in _bf16_split(c))
    packed = hi + pltpu.roll(mid, N_HEADS, 1) + pltpu.roll(lo, 2 * N_HEADS, 1)
    placed = jnp.dot(packed.astype(BF16), place_ref[...], preferred_element_type=F32)

    slab_lane = lax.broadcasted_iota(jnp.int32, (tm, d), 1) % LANES
    first = slab_lane < HEAD_DIM
    q = proj[:, :d] * (HEAD_DIM ** -0.5 * LOG2E)
    k = proj[:, d:2 * d]
    nb = BIAS_PIECES

    def bias_lanes(lo_lane, a, b):
        return jnp.where(slab_lane < lo_lane + nb, a, jnp.where(slab_lane < lo_lane + 2 * nb, b, 0.0))

    qe_ref[...] = jnp.where(first, q, bias_lanes(HEAD_DIM, 1.0, placed)).astype(BF16)
    ke_ref[...] = jnp.where(first, k, bias_lanes(HEAD_DIM, -placed, 1.0)).astype(BF16)
    qo_ref[...] = jnp.where(first, bias_lanes(0, 1.0, placed), q).astype(BF16)
    ko_ref[...] = jnp.where(first, bias_lanes(0, -placed, 1.0), k).astype(BF16)

    vt = proj[:, 2 * d:3 * d].T
    extra = jnp.where(lax.broadcasted_iota(jnp.int32, (VT_ROWS - HEAD_DIM, tm), 0) == 0, 1.0, 0.0)
    rows = []
    for h in range(N_HEADS):
        rows += [vt[h * HEAD_DIM:(h + 1) * HEAD_DIM], extra]
    vt_ref[...] = jnp.concatenate(rows, axis=0).astype(BF16)


def _inproj(h, w_all, bfp, c0, place, *, tm):
    bsz, s, d = h.shape
    n_out = w_all.shape[1]
    row = pl.BlockSpec((None, tm, d), lambda bi, j: (bi, j, 0))
    act = jax.ShapeDtypeStruct((bsz, s, d), BF16)
    return pl.pallas_call(
        functools.partial(_inproj_kernel, tm=tm),
        grid=(bsz, s // tm),
        in_specs=[row, _const_spec((d, n_out), True), _const_spec((1, LANES)), _const_spec((1, LANES)),
                  _const_spec((LANES, d))],
        out_specs=[row, row, row, row,
                   pl.BlockSpec((None, N_HEADS * VT_ROWS, tm), lambda bi, j: (bi, 0, j)),
                   pl.BlockSpec((None, tm, LANES), lambda bi, j: (bi, j, 0))],
        out_shape=[act, act, act, act,
                   jax.ShapeDtypeStruct((bsz, N_HEADS * VT_ROWS, s), BF16),
                   jax.ShapeDtypeStruct((bsz, s, LANES), F32)],
        scratch_shapes=[pltpu.VMEM((1, LANES), F32)],
        compiler_params=pltpu.CompilerParams(dimension_semantics=("arbitrary", "arbitrary"),
                                             vmem_limit_bytes=56 << 20),
        name=f"inproj_{bsz * s}",
    )(h, w_all, bfp, c0, place)


def _attn_kernel(qe_ref, qo_ref, ke_ref, ko_ref, vt_ref, kme_ref, kmo_ref, vtm_ref, o_ref,
                 s_sc, mb_sc, m_sc, acc_sc, *, tq):
    nq = o_ref.shape[0] // tq
    q_refs = (qe_ref, qo_ref)
    k_refs = (ke_ref, ko_ref)
    km_refs = (kme_ref, kmo_ref)

    def rows(i):
        return pl.ds(pl.multiple_of(i * tq, tq), tq)

    def dot_nt(a, b):
        return lax.dot_general(a, b, (((1,), (1,)), ((), ())), preferred_element_type=F32)

    def score(hh, jq, kc):
        st = dot_nt(k_refs[hh][rows(kc), :], q_refs[hh][rows(jq), :])
        s_sc[hh] = st
        mb_sc[hh] = jnp.max(st, axis=0, keepdims=True)

    def accumulate(hh, m_new, parts, vts):
        pv = None
        for st, vt in zip(parts, vts):
            term = jnp.dot(vt, jnp.exp2(st - m_new).astype(BF16), preferred_element_type=F32)
            pv = term if pv is None else pv + term
        acc_sc[hh] = jnp.exp2(m_sc[hh] - m_new) * acc_sc[hh] + pv
        m_sc[hh] = m_new

    def vt_rows(ref, hh, cols):
        return ref[hh * VT_ROWS:(hh + 1) * VT_ROWS, cols]

    def consume(hh, kc):
        accumulate(hh, jnp.maximum(m_sc[hh], mb_sc[hh]), [s_sc[hh]], [vt_rows(vt_ref, hh, rows(kc))])

    causal_t = (lax.broadcasted_iota(jnp.int32, (tq, tq), 0)
                <= lax.broadcasted_iota(jnp.int32, (tq, tq), 1))
    is_meta = lax.broadcasted_iota(jnp.int32, (META_ROWS, tq), 0) < N_META

    def consume_diag(hh, jq):
        sd = jnp.where(causal_t, s_sc[hh], MASK_VALUE)
        sm = jnp.where(is_meta, dot_nt(km_refs[hh][...], q_refs[hh][rows(jq), :]), MASK_VALUE)
        m_blk = jnp.maximum(jnp.max(sd, axis=0, keepdims=True), jnp.max(sm, axis=0, keepdims=True))
        accumulate(hh, jnp.maximum(m_sc[hh], m_blk), [sd, sm],
                   [vt_rows(vt_ref, hh, rows(jq)), vt_rows(vtm_ref, hh, slice(None))])
        acc = acc_sc[hh]
        m_sc[hh] = jnp.full((1, tq), MASK_VALUE, F32)
        acc_sc[hh] = jnp.zeros((VT_ROWS, tq), F32)
        return acc[:HEAD_DIM] / acc[HEAD_DIM:HEAD_DIM + 1]

    m_sc[...] = jnp.full(m_sc.shape, MASK_VALUE, F32)
    acc_sc[...] = jnp.zeros(acc_sc.shape, F32)
    score(0, 0, 0)

    def tile(jq, carry):
        def chunk(kc, c):
            score(1, jq, kc)
            consume(0, kc)
            score(0, jq, kc + 1)
            consume(1, kc)
            return c

        lax.fori_loop(0, jq, chunk, 0)
        score(1, jq, jq)
        out0 = consume_diag(0, jq)
        score(0, jnp.minimum(jq + 1, nq - 1), 0)
        out1 = consume_diag(1, jq)
        o_ref[rows(jq), :] = jnp.concatenate([out0, out1], axis=0).T.astype(BF16)
        return carry

    lax.fori_loop(0, nq, tile, 0)


def _attention(qe, qo, ke, ko, vt, kme, kmo, vtm, *, tq):
    bsz, s, d = ke.shape
    pair_rows = 2 * VT_ROWS
    seq = pl.BlockSpec((None, s, LANES), lambda bi, hp: (bi, 0, hp))
    kmspec = pl.BlockSpec((META_ROWS, LANES), lambda bi, hp: (0, hp))
    return pl.pallas_call(
        functools.partial(_attn_kernel, tq=tq),
        grid=(bsz, d // LANES),
        in_specs=[seq, seq, seq, seq,
                  pl.BlockSpec((None, pair_rows, s), lambda bi, hp: (bi, hp, 0)),
                  kmspec, kmspec,
                  pl.BlockSpec((pair_rows, META_ROWS), lambda bi, hp: (hp, 0))],
        out_specs=seq,
        out_shape=jax.ShapeDtypeStruct((bsz, s, d), BF16),
        scratch_shapes=[pltpu.VMEM((2, tq, tq), F32), pltpu.VMEM((2, 1, tq), F32),
                        pltpu.VMEM((2, 1, tq), F32), pltpu.VMEM((2, VT_ROWS, tq), F32)],
        compiler_params=pltpu.CompilerParams(dimension_semantics=("arbitrary", "arbitrary"),
                                             vmem_limit_bytes=48 << 20),
        name="fox_attn",
    )(qe, qo, ke, ko, vt, kme, kmo, vtm)


def kernel(x, meta_tokens, pool_w, pool_scale, fox_w_in, fox_b_f, fox_w_o, ffn_w_gate, ffn_w_up,
           ffn_w_down, ln_g, ln_b):
    bsz, s, d = x.shape
    f_dim = ffn_w_gate.shape[-1]
    tm = 512
    tq = 512
    fc = f_dim // 2
    row = lambda a: a.reshape(1, -1).astype(F32)

    meta_pad = jnp.pad(meta_tokens, ((0, META_ROWS - N_META), (0, 0)))
    wg = ffn_w_gate.astype(BF16)
    wu = ffn_w_up.astype(BF16)
    wd = ffn_w_down.astype(BF16)

    h1x, h1m = _pool_ln(x, meta_tokens, meta_pad, pool_w[0].astype(BF16), row(pool_scale[0]),
                        row(ln_g[0, 0]), row(ln_b[0, 0]), tm=tm)
    ffn0 = functools.partial(_ffn_ln, wg=wg[0], wu=wu[0], wd=wd[0], g=row(ln_g[0, 1]), b=row(ln_b[0, 1]), fc=fc)
    h2x = ffn0(h1x.reshape(bsz * s, d), tm=tm)
    h2m = ffn0(h1m, tm=META_ROWS)

    w_all = jnp.pad(fox_w_in[0], ((0, 0), (0, LANES - N_HEADS))).astype(BF16)
    bfp = jnp.pad(fox_b_f[0], (0, LANES - N_HEADS)).reshape(1, LANES)
    place = _bias_placement(d)
    _, _, kme, kmo, vtm, cm = _inproj(h2m[None], w_all, bfp, jnp.zeros((1, LANES), F32), place, tm=META_ROWS)
    qe, qo, ke, ko, vt, _ = _inproj(h2x.reshape(bsz, s, d), w_all, bfp, cm[0, N_META - 1:N_META, :], place, tm=tm)
    o = _attention(qe, qo, ke, ko, vt, kme[0], kmo[0], vtm[0], tq=tq)
    out = _proj_ffn_ln(o.reshape(bsz * s, d), h2x, fox_w_o[0].astype(BF16), row(ln_g[1, 0]), row(ln_b[1, 0]),
                       wg[1], wu[1], wd[1], row(ln_g[1, 1]), row(ln_b[1, 1]), tm=tm, fc=fc)
    return out.reshape(bsz, s, d)
```

```python
import functools
import math

import jax
import jax.numpy as jnp
import numpy as np
from jax import lax
from jax.experimental import pallas as pl
from jax.experimental.pallas import tpu as pltpu

N_META = 16
N_HEADS = 16
HEAD_DIM = 64
POOL_WINDOWS = (2, 4, 8, 16)
MAX_WINDOW = max(POOL_WINDOWS)
DEPTH = 2
DN_ALPHA = (2.0 * DEPTH) ** 0.25
LN_EPS = 1e-5
LANES = 128
META_ROWS = 128
MASK_VALUE = -1e30
LOG2E = math.log2(math.e)

F32 = jnp.float32
BF16 = jnp.bfloat16


def _layer_norm(z, g, b):
    mu = jnp.mean(z, axis=-1, keepdims=True)
    zc = z - mu
    var = jnp.mean(zc * zc, axis=-1, keepdims=True)
    return zc * lax.rsqrt(var + LN_EPS) * g + b


def _const_spec(shape, single_buffer=False):
    zeros = (0,) * len(shape)
    if single_buffer:
        return pl.BlockSpec(shape, lambda *_: zeros, pipeline_mode=pl.Buffered(1))
    return pl.BlockSpec(shape, lambda *_: zeros)


def _pool_ln_kernel(x_ref, halo_ref, meta_ref, w_ref, sc_ref, g_ref, b_ref, o_ref, ext_ref,
                    *, tm, is_meta):
    d = x_ref.shape[-1]
    cgrp = d // len(POOL_WINDOWS)
    cur = x_ref[...]
    ext_ref[MAX_WINDOW:, :] = cur
    if is_meta:
        ext_ref[:MAX_WINDOW, :] = jnp.zeros((MAX_WINDOW, d), F32)
        pos0 = 0
    else:
        j = pl.program_id(1)

        @pl.when(j == 0)
        def _():
            ext_ref[:MAX_WINDOW, :] = meta_ref[...]

        @pl.when(j > 0)
        def _():
            ext_ref[:MAX_WINDOW, :] = halo_ref[...]

        pos0 = N_META + j * tm
    t1 = (pos0 + 1 + lax.broadcasted_iota(jnp.int32, (tm, 1), 0)).astype(F32)
    zs = []
    for g, w in enumerate(POOL_WINDOWS):
        sl = slice(g * cgrp, (g + 1) * cgrp)
        cur_g = cur[:, sl]
        win = cur_g
        for i in range(1, w):
            win = win + ext_ref[pl.ds(MAX_WINDOW - i, tm), sl]
        cnt = jnp.minimum(t1, float(w))
        y = win / cnt - cur_g
        m = jnp.dot(y.astype(BF16), w_ref[g], preferred_element_type=F32) * sc_ref[:, sl]
        zs.append(DN_ALPHA * cur_g + m)
    z = jnp.concatenate(zs, axis=1)
    o_ref[...] = _layer_norm(z, g_ref[...], b_ref[...])


def _pool_ln(x, meta_tokens, meta_pad, pw, psc, g, b, *, tm):
    bsz, s, d = x.shape
    ngrp = len(POOL_WINDOWS)
    cgrp = d // ngrp
    hb = tm // MAX_WINDOW
    consts = [_const_spec((ngrp, cgrp, cgrp)), _const_spec((1, d)), _const_spec((1, d)), _const_spec((1, d))]
    hx = pl.pallas_call(
        functools.partial(_pool_ln_kernel, tm=tm, is_meta=False),
        grid=(bsz, s // tm),
        in_specs=[pl.BlockSpec((None, tm, d), lambda bi, j: (bi, j, 0)),
                  pl.BlockSpec((None, MAX_WINDOW, d), lambda bi, j: (bi, jnp.maximum(j * hb - 1, 0), 0)),
                  _const_spec((N_META, d))] + consts,
        out_specs=pl.BlockSpec((None, tm, d), lambda bi, j: (bi, j, 0)),
        out_shape=jax.ShapeDtypeStruct((bsz, s, d), F32),
        scratch_shapes=[pltpu.VMEM((MAX_WINDOW + tm, d), F32)],
        compiler_params=pltpu.CompilerParams(dimension_semantics=("arbitrary", "arbitrary")),
        name="pool_ln_x",
    )(x, x, meta_tokens, pw, psc, g, b)
    hm = pl.pallas_call(
        functools.partial(_pool_ln_kernel, tm=META_ROWS, is_meta=True),
        grid=(1,),
        in_specs=[_const_spec((META_ROWS, d)), _const_spec((MAX_WINDOW, d)), _const_spec((N_META, d))] + consts,
        out_specs=_const_spec((META_ROWS, d)),
        out_shape=jax.ShapeDtypeStruct((META_ROWS, d), F32),
        scratch_shapes=[pltpu.VMEM((MAX_WINDOW + META_ROWS, d), F32)],
        name="pool_ln_meta",
    )(meta_pad, meta_tokens, meta_tokens, pw, psc, g, b)
    return hx, hm


def _swiglu(hb, wg_ref, wu_ref, wd_ref, fc):
    f_dim = wg_ref.shape[1]
    out = None
    for c in range(f_dim // fc):
        sl = slice(c * fc, (c + 1) * fc)
        hg = jnp.dot(hb, wg_ref[:, sl], preferred_element_type=F32)
        hu = jnp.dot(hb, wu_ref[:, sl], preferred_element_type=F32)
        a = (jax.nn.silu(hg) * hu).astype(BF16)
        part = jnp.dot(a, wd_ref[sl, :], preferred_element_type=F32)
        out = part if out is None else out + part
    return out


def _ffn_ln_kernel(h_ref, wg_ref, wu_ref, wd_ref, g_ref, b_ref, o_ref, *, fc):
    h = h_ref[...]
    f = _swiglu(h.astype(BF16), wg_ref, wu_ref, wd_ref, fc)
    o_ref[...] = _layer_norm(DN_ALPHA * h + f, g_ref[...], b_ref[...])


def _proj_ffn_ln_kernel(a_ref, h_ref, wo_ref, g1_ref, b1_ref, wg_ref, wu_ref, wd_ref, g2_ref, b2_ref,
                        o_ref, *, fc):
    m = jnp.dot(a_ref[...], wo_ref[...], preferred_element_type=F32)
    h = _layer_norm(DN_ALPHA * h_ref[...] + m, g1_ref[...], b1_ref[...])
    f = _swiglu(h.astype(BF16), wg_ref, wu_ref, wd_ref, fc)
    o_ref[...] = _layer_norm(DN_ALPHA * h + f, g2_ref[...], b2_ref[...])


def _ffn_vmem_limit(tm, d, f, fc, with_proj):
    weights = 3 * d * f * 2 + (d * d * 2 if with_proj else 0)
    tiles = 2 * 2 * tm * d * 4 + (2 * tm * d * 2 + 2 * tm * d * 4 if with_proj else 0)
    temps = tm * fc * (4 + 4 + 4 + 2) + 4 * tm * d * 4
    return weights + tiles + temps + (8 << 20)


def _ffn_ln(h, wg, wu, wd, g, b, *, tm, fc):
    n, d = h.shape
    f = wg.shape[1]
    row = pl.BlockSpec((tm, d), lambda i: (i, 0))
    return pl.pallas_call(
        functools.partial(_ffn_ln_kernel, fc=fc),
        grid=(n // tm,),
        in_specs=[row, _const_spec((d, f), True), _const_spec((d, f), True), _const_spec((f, d), True),
                  _const_spec((1, d)), _const_spec((1, d))],
        out_specs=row,
        out_shape=jax.ShapeDtypeStruct((n, d), F32),
        compiler_params=pltpu.CompilerParams(dimension_semantics=("arbitrary",),
                                             vmem_limit_bytes=_ffn_vmem_limit(tm, d, f, fc, False)),
        name=f"ffn_ln_{n}",
    )(h, wg, wu, wd, g, b)


def _proj_ffn_ln(a, h, wo, g1, b1, wg, wu, wd, g2, b2, *, tm, fc):
    n, d = h.shape
    f = wg.shape[1]
    row = pl.BlockSpec((tm, d), lambda i: (i, 0))
    return pl.pallas_call(
        functools.partial(_proj_ffn_ln_kernel, fc=fc),
        grid=(n // tm,),
        in_specs=[row, row, _const_spec((d, d), True), _const_spec((1, d)), _const_spec((1, d)),
                  _const_spec((d, f), True), _const_spec((d, f), True), _const_spec((f, d), True),
                  _const_spec((1, d)), _const_spec((1, d))],
        out_specs=row,
        out_shape=jax.ShapeDtypeStruct((n, d), F32),
        compiler_params=pltpu.CompilerParams(dimension_semantics=("arbitrary",),
                                             vmem_limit_bytes=_ffn_vmem_limit(tm, d, f, fc, True)),
        name="proj_ffn_ln",
    )(a, h, wo, g1, b1, wg, wu, wd, g2, b2)


BIAS_PIECES = 3
VT_ROWS = 80


def _bf16_split(x):
    hi = x.astype(BF16)
    r1 = x - hi.astype(F32)
    mid = r1.astype(BF16)
    lo = (r1 - mid.astype(F32)).astype(BF16)
    return hi, mid, lo


def _bias_placement(d):
    p = np.zeros((LANES, d), np.float32)
    for h in range(N_HEADS):
        base = (h // 2) * LANES + (HEAD_DIM if h % 2 == 0 else 0)
        for piece in range(BIAS_PIECES):
            p[piece * N_HEADS + h, base + piece] = 1.0
            p[piece * N_HEADS + h, base + BIAS_PIECES + piece] = 1.0
    return jnp.asarray(p, BF16)


def _inproj_kernel(h_ref, w_ref, bf_ref, c0_ref, place_ref, qe_ref, qo_ref, ke_ref, ko_ref, vt_ref, c_ref,
                   carry_ref, *, tm):
    d = h_ref.shape[-1]

    @pl.when(pl.program_id(1) == 0)
    def _():
        carry_ref[...] = c0_ref[...]

    proj = jnp.dot(h_ref[...].astype(BF16), w_ref[...], preferred_element_type=F32)
    z = proj[:, 3 * d:3 * d + LANES] + bf_ref[...]
    logf = (jnp.minimum(z, 0.0) - jnp.log1p(jnp.exp(-jnp.abs(z)))) * LOG2E
    tri = (lax.broadcasted_iota(jnp.int32, (tm, tm), 0)
           >= lax.broadcasted_iota(jnp.int32, (tm, tm), 1)).astype(BF16)
    cum = sum(jnp.dot(tri, piece, preferred_element_type=F32) for piece in _bf16_split(logf))
    c = carry_ref[...] + cum
    c_ref[...] = c
    carry_ref[...] = c[tm - 1:tm, :]

    head_lane = lax.broadcasted_iota(jnp.int32, (tm, LANES), 1) < N_HEADS
    hi, mid, lo = (jnp.where(head_lane, piece.astype(F32), 0.0) for piece in _bf16_split(c))
    packed = hi + pltpu.roll(mid, N_HEADS, 1) + pltpu.roll(lo, 2 * N_HEADS, 1)
    placed = jnp.dot(packed.astype(BF16), place_ref[...], preferred_element_type=F32)

    slab_lane = lax.broadcasted_iota(jnp.int32, (tm, d), 1) % LANES
    first = slab_lane < HEAD_DIM
    q = proj[:, :d] * (HEAD_DIM ** -0.5 * LOG2E)
    k = proj[:, d:2 * d]
    nb = BIAS_PIECES

    def bias_lanes(lo_lane, a, b):
        return jnp.where(slab_lane < lo_lane + nb, a, jnp.where(slab_lane < lo_lane + 2 * nb, b, 0.0))

    qe_ref[...] = jnp.where(first, q, bias_lanes(HEAD_DIM, 1.0, placed)).astype(BF16)
    ke_ref[...] = jnp.where(first, k, bias_lanes(HEAD_DIM, -placed, 1.0)).astype(BF16)
    qo_ref[...] = jnp.where(first, bias_lanes(0, 1.0, placed), q).astype(BF16)
    ko_ref[...] = jnp.where(first, bias_lanes(0, -placed, 1.0), k).astype(BF16)

    vt = proj[:, 2 * d:3 * d].T
    extra = jnp.where(lax.broadcasted_iota(jnp.int32, (VT_ROWS - HEAD_DIM, tm), 0) == 0, 1.0, 0.0)
    rows = []
    for h in range(N_HEADS):
        rows += [vt[h * HEAD_DIM:(h + 1) * HEAD_DIM], extra]
    vt_ref[...] = jnp.concatenate(rows, axis=0).astype(BF16)


def _inproj(h, w_all, bfp, c0, place, *, tm):
    bsz, s, d = h.shape
    n_out = w_all.shape[1]
    row = pl.BlockSpec((None, tm, d), lambda bi, j: (bi, j, 0))
    act = jax.ShapeDtypeStruct((bsz, s, d), BF16)
    return pl.pallas_call(
        functools.partial(_inproj_kernel, tm=tm),
        grid=(bsz, s // tm),
        in_specs=[row, _const_spec((d, n_out), True), _const_spec((1, LANES)), _const_spec((1, LANES)),
                  _const_spec((LANES, d))],
        out_specs=[row, row, row, row,
                   pl.BlockSpec((None, N_HEADS * VT_ROWS, tm), lambda bi, j: (bi, 0, j)),
                   pl.BlockSpec((None, tm, LANES), lambda bi, j: (bi, j, 0))],
        out_shape=[act, act, act, act,
                   jax.ShapeDtypeStruct((bsz, N_HEADS * VT_ROWS, s), BF16),
                   jax.ShapeDtypeStruct((bsz, s, LANES), F32)],
        scratch_shapes=[pltpu.VMEM((1, LANES), F32)],
        compiler_params=pltpu.CompilerParams(dimension_semantics=("arbitrary", "arbitrary"),
                                             vmem_limit_bytes=56 << 20),
        name=f"inproj_{bsz * s}",
    )(h, w_all, bfp, c0, place)


def _attn_kernel(qe_ref, qo_ref, ke_ref, ko_ref, vt_ref, kme_ref, kmo_ref, vtm_ref, o_ref,
                 s_sc, mb_sc, p_sc, pm_sc, al_sc, m_sc, acc_sc, *, tq):
    nq = o_ref.shape[0] // tq
    nh = acc_sc.shape[0]

    def rows(i):
        return pl.ds(pl.multiple_of(i * tq, tq), tq)

    def slab(g):
        return slice((g // 2) * LANES, (g // 2 + 1) * LANES)

    def q_tile(g, jq):
        return (qe_ref, qo_ref)[g % 2][rows(jq), slab(g)]

    def k_chunk(g, kc):
        return (ke_ref, ko_ref)[g % 2][rows(kc), slab(g)]

    def k_meta(g):
        return (kme_ref, kmo_ref)[g % 2][:, slab(g)]

    def vt_rows(ref, g, cols):
        return ref[g * VT_ROWS:(g + 1) * VT_ROWS, cols]

    def dot_nt(a, b):
        return lax.dot_general(a, b, (((1,), (1,)), ((), ())), preferred_element_type=F32)

    causal_t = (lax.broadcasted_iota(jnp.int32, (tq, tq), 0)
                <= lax.broadcasted_iota(jnp.int32, (tq, tq), 1))
    is_meta = lax.broadcasted_iota(jnp.int32, (META_ROWS, tq), 0) < N_META

    def stage_a(jq, kc):
        for g in range(nh):
            st = dot_nt(k_chunk(g, kc), q_tile(g, jq))
            s_sc[g] = st
            mb_sc[g] = jnp.max(st, axis=0, keepdims=True)

    def stage_b(jq, first, diag):
        for g in range(nh):
            st = s_sc[g]
            if diag:
                st = jnp.where(causal_t, st, MASK_VALUE)
                sm = jnp.where(is_meta, dot_nt(k_meta(g), q_tile(g, jq)), MASK_VALUE)
                m_blk = jnp.maximum(jnp.max(st, axis=0, keepdims=True), jnp.max(sm, axis=0, keepdims=True))
            else:
                m_blk = mb_sc[g]
            if first:
                m_new = m_blk
                al_sc[g] = jnp.zeros((1, tq), F32)
            else:
                m_old = m_sc[g]
                m_new = jnp.maximum(m_old, m_blk)
                al_sc[g] = jnp.exp2(m_old - m_new)
            m_sc[g] = m_new
            p_sc[g] = jnp.exp2(st - m_new).astype(BF16)
            if diag:
                pm_sc[g] = jnp.exp2(sm - m_new).astype(BF16)

    def stage_c(jq, kc, diag):
        outs = []
        for g in range(nh):
            pv = jnp.dot(vt_rows(vt_ref, g, rows(kc)), p_sc[g], preferred_element_type=F32)
            if diag:
                pv = pv + jnp.dot(vt_rows(vtm_ref, g, slice(None)), pm_sc[g], preferred_element_type=F32)
            acc = al_sc[g] * acc_sc[g] + pv
            acc_sc[g] = acc
            if diag:
                outs.append(acc[:HEAD_DIM] / acc[HEAD_DIM:HEAD_DIM + 1])
        for g in range(0, len(outs), 2):
            o_ref[rows(jq), slab(g)] = jnp.concatenate(outs[g:g + 2], axis=0).T.astype(BF16)

    acc_sc[...] = jnp.zeros(acc_sc.shape, F32)
    stage_a(0, 0)
    stage_b(0, True, True)
    stage_a(1, 0)

    def tile(jq, carry):
        stage_c(jq - 1, jq - 1, True)
        stage_b(jq, True, False)
        stage_a(jq, 1)

        def chunk(kc, c):
            stage_c(jq, kc - 1, False)
            stage_b(jq, False, False)
            stage_a(jq, kc + 1)
            return c

        lax.fori_loop(1, jq, chunk, 0)
        stage_c(jq, jq - 1, False)
        stage_b(jq, False, True)
        stage_a(jnp.minimum(jq + 1, nq - 1), 0)
        return carry

    lax.fori_loop(1, nq, tile, 0)
    stage_c(nq - 1, nq - 1, True)


def _attention(qe, qo, ke, ko, vt, kme, kmo, vtm, *, tq, npair):
    bsz, s, d = ke.shape
    nh = 2 * npair
    width = npair * LANES
    seq = pl.BlockSpec((None, s, width), lambda bi, hp: (bi, 0, hp))
    kmspec = pl.BlockSpec((META_ROWS, width), lambda bi, hp: (0, hp))
    return pl.pallas_call(
        functools.partial(_attn_kernel, tq=tq),
        grid=(bsz, d // width),
        in_specs=[seq, seq, seq, seq,
                  pl.BlockSpec((None, nh * VT_ROWS, s), lambda bi, hp: (bi, hp, 0)),
                  kmspec, kmspec,
                  pl.BlockSpec((nh * VT_ROWS, META_ROWS), lambda bi, hp: (hp, 0))],
        out_specs=seq,
        out_shape=jax.ShapeDtypeStruct((bsz, s, d), BF16),
        scratch_shapes=[pltpu.VMEM((nh, tq, tq), F32), pltpu.VMEM((nh, 1, tq), F32),
                        pltpu.VMEM((nh, tq, tq), BF16), pltpu.VMEM((nh, META_ROWS, tq), BF16),
                        pltpu.VMEM((nh, 1, tq), F32), pltpu.VMEM((nh, 1, tq), F32),
                        pltpu.VMEM((nh, VT_ROWS, tq), F32)],
        compiler_params=pltpu.CompilerParams(dimension_semantics=("arbitrary", "arbitrary"),
                                             vmem_limit_bytes=56 << 20),
        name="fox_attn",
    )(qe, qo, ke, ko, vt, kme, kmo, vtm)


def kernel(x, meta_tokens, pool_w, pool_scale, fox_w_in, fox_b_f, fox_w_o, ffn_w_gate, ffn_w_up,
           ffn_w_down, ln_g, ln_b):
    bsz, s, d = x.shape
    f_dim = ffn_w_gate.shape[-1]
    tm = 512
    tq = 512
    fc = f_dim // 2
    row = lambda a: a.reshape(1, -1).astype(F32)

    meta_pad = jnp.pad(meta_tokens, ((0, META_ROWS - N_META), (0, 0)))
    wg = ffn_w_gate.astype(BF16)
    wu = ffn_w_up.astype(BF16)
    wd = ffn_w_down.astype(BF16)

    h1x, h1m = _pool_ln(x, meta_tokens, meta_pad, pool_w[0].astype(BF16), row(pool_scale[0]),
                        row(ln_g[0, 0]), row(ln_b[0, 0]), tm=tm)
    ffn0 = functools.partial(_ffn_ln, wg=wg[0], wu=wu[0], wd=wd[0], g=row(ln_g[0, 1]), b=row(ln_b[0, 1]), fc=fc)
    h2x = ffn0(h1x.reshape(bsz * s, d), tm=tm)
    h2m = ffn0(h1m, tm=META_ROWS)

    w_all = jnp.pad(fox_w_in[0], ((0, 0), (0, LANES - N_HEADS))).astype(BF16)
    bfp = jnp.pad(fox_b_f[0], (0, LANES - N_HEADS)).reshape(1, LANES)
    place = _bias_placement(d)
    _, _, kme, kmo, vtm, cm = _inproj(h2m[None], w_all, bfp, jnp.zeros((1, LANES), F32), place, tm=META_ROWS)
    qe, qo, ke, ko, vt, _ = _inproj(h2x.reshape(bsz, s, d), w_all, bfp, cm[0, N_META - 1:N_META, :], place, tm=tm)
    o = _attention(qe, qo, ke, ko, vt, kme[0], kmo[0], vtm[0], tq=tq, npair=2)
    out = _proj_ffn_ln(o.reshape(bsz * s, d), h2x, fox_w_o[0].astype(BF16), row(ln_g[1, 0]), row(ln_b[1, 0]),
                       wg[1], wu[1], wd[1], row(ln_g[1, 1]), row(ln_b[1, 1]), tm=tm, fc=fc)
    return out.reshape(bsz, s, d)
```

```python
import functools
import math

import jax
import jax.numpy as jnp
import numpy as np
from jax import lax
from jax.experimental import pallas as pl
from jax.experimental.pallas import tpu as pltpu

N_META = 16
N_HEADS = 16
HEAD_DIM = 64
POOL_WINDOWS = (2, 4, 8, 16)
MAX_WINDOW = max(POOL_WINDOWS)
DEPTH = 2
DN_ALPHA = (2.0 * DEPTH) ** 0.25
LN_EPS = 1e-5
LANES = 128
MXU_TILE = 256
META_ROWS = 128
MASK_VALUE = -1e30
LOG2E = math.log2(math.e)

F32 = jnp.float32
BF16 = jnp.bfloat16


def _layer_norm(z, g, b):
    mu = jnp.mean(z, axis=-1, keepdims=True)
    zc = z - mu
    var = jnp.mean(zc * zc, axis=-1, keepdims=True)
    return zc * lax.rsqrt(var + LN_EPS) * g + b


def _const_spec(shape, single_buffer=False):
    zeros = (0,) * len(shape)
    if single_buffer:
        return pl.BlockSpec(shape, lambda *_: zeros, pipeline_mode=pl.Buffered(1))
    return pl.BlockSpec(shape, lambda *_: zeros)


def _pool_ln_kernel(x_ref, halo_ref, meta_ref, w_ref, sc_ref, g_ref, b_ref, o_ref, ext_ref,
                    *, tm, is_meta):
    d = x_ref.shape[-1]
    cgrp = d // len(POOL_WINDOWS)
    cur = x_ref[...]
    ext_ref[MAX_WINDOW:, :] = cur
    if is_meta:
        ext_ref[:MAX_WINDOW, :] = jnp.zeros((MAX_WINDOW, d), F32)
        pos0 = 0
    else:
        j = pl.program_id(1)

        @pl.when(j == 0)
        def _():
            ext_ref[:MAX_WINDOW, :] = meta_ref[...]

        @pl.when(j > 0)
        def _():
            ext_ref[:MAX_WINDOW, :] = halo_ref[...]

        pos0 = N_META + j * tm
    t1 = (pos0 + 1 + lax.broadcasted_iota(jnp.int32, (tm, 1), 0)).astype(F32)
    zs = []
    for g, w in enumerate(POOL_WINDOWS):
        sl = slice(g * cgrp, (g + 1) * cgrp)
        cur_g = cur[:, sl]
        win = cur_g
        for i in range(1, w):
            win = win + ext_ref[pl.ds(MAX_WINDOW - i, tm), sl]
        cnt = jnp.minimum(t1, float(w))
        y = win / cnt - cur_g
        m = jnp.dot(y.astype(BF16), w_ref[g], preferred_element_type=F32) * sc_ref[:, sl]
        zs.append(DN_ALPHA * cur_g + m)
    z = jnp.concatenate(zs, axis=1)
    o_ref[...] = _layer_norm(z, g_ref[...], b_ref[...])


def _pool_ln(x, meta_tokens, meta_pad, pw, psc, g, b, *, tm):
    bsz, s, d = x.shape
    ngrp = len(POOL_WINDOWS)
    cgrp = d // ngrp
    hb = tm // MAX_WINDOW
    consts = [_const_spec((ngrp, cgrp, cgrp)), _const_spec((1, d)), _const_spec((1, d)), _const_spec((1, d))]
    hx = pl.pallas_call(
        functools.partial(_pool_ln_kernel, tm=tm, is_meta=False),
        grid=(bsz, s // tm),
        in_specs=[pl.BlockSpec((None, tm, d), lambda bi, j: (bi, j, 0)),
                  pl.BlockSpec((None, MAX_WINDOW, d), lambda bi, j: (bi, jnp.maximum(j * hb - 1, 0), 0)),
                  _const_spec((N_META, d))] + consts,
        out_specs=pl.BlockSpec((None, tm, d), lambda bi, j: (bi, j, 0)),
        out_shape=jax.ShapeDtypeStruct((bsz, s, d), F32),
        scratch_shapes=[pltpu.VMEM((MAX_WINDOW + tm, d), F32)],
        compiler_params=pltpu.CompilerParams(dimension_semantics=("arbitrary", "arbitrary")),
        name="pool_ln_x",
    )(x, x, meta_tokens, pw, psc, g, b)
    hm = pl.pallas_call(
        functools.partial(_pool_ln_kernel, tm=META_ROWS, is_meta=True),
        grid=(1,),
        in_specs=[_const_spec((META_ROWS, d)), _const_spec((MAX_WINDOW, d)), _const_spec((N_META, d))] + consts,
        out_specs=_const_spec((META_ROWS, d)),
        out_shape=jax.ShapeDtypeStruct((META_ROWS, d), F32),
        scratch_shapes=[pltpu.VMEM((MAX_WINDOW + META_ROWS, d), F32)],
        name="pool_ln_meta",
    )(meta_pad, meta_tokens, meta_tokens, pw, psc, g, b)
    return hx, hm


def _swiglu(hb, wg_ref, wu_ref, wd_ref, fc):
    f_dim = wg_ref.shape[1]
    out = None
    for lo in range(0, f_dim, fc):
        sl = slice(lo, min(lo + fc, f_dim))
        hg = jnp.dot(hb, wg_ref[:, sl], preferred_element_type=F32)
        hu = jnp.dot(hb, wu_ref[:, sl], preferred_element_type=F32)
        a = (jax.nn.silu(hg) * hu).astype(BF16)
        part = jnp.dot(a, wd_ref[sl, :], preferred_element_type=F32)
        out = part if out is None else out + part
    return out


def _ffn_ln_kernel(h_ref, wg_ref, wu_ref, wd_ref, g_ref, b_ref, o_ref, *, fc):
    h = h_ref[...]
    f = _swiglu(h.astype(BF16), wg_ref, wu_ref, wd_ref, fc)
    o_ref[...] = _layer_norm(DN_ALPHA * h + f, g_ref[...], b_ref[...])


def _proj_ffn_ln_kernel(a_ref, h_ref, wo_ref, g1_ref, b1_ref, wg_ref, wu_ref, wd_ref, g2_ref, b2_ref,
                        o_ref, *, fc):
    m = jnp.dot(a_ref[...], wo_ref[...], preferred_element_type=F32)
    h = _layer_norm(DN_ALPHA * h_ref[...] + m, g1_ref[...], b1_ref[...])
    f = _swiglu(h.astype(BF16), wg_ref, wu_ref, wd_ref, fc)
    o_ref[...] = _layer_norm(DN_ALPHA * h + f, g2_ref[...], b2_ref[...])


def _ffn_vmem_limit(tm, d, f, fc, with_proj):
    weights = 3 * d * f * 2 + (d * d * 2 if with_proj else 0)
    tiles = 2 * 2 * tm * d * 4 + (2 * tm * d * 2 + 2 * tm * d * 4 if with_proj else 0)
    temps = tm * fc * (4 + 4 + 4 + 2) + 4 * tm * d * 4
    return weights + tiles + temps + (8 << 20)


def _ffn_ln(h, wg, wu, wd, g, b, *, tm, fc):
    n, d = h.shape
    f = wg.shape[1]
    row = pl.BlockSpec((tm, d), lambda i: (i, 0))
    return pl.pallas_call(
        functools.partial(_ffn_ln_kernel, fc=fc),
        grid=(n // tm,),
        in_specs=[row, _const_spec((d, f), True), _const_spec((d, f), True), _const_spec((f, d), True),
                  _const_spec((1, d)), _const_spec((1, d))],
        out_specs=row,
        out_shape=jax.ShapeDtypeStruct((n, d), F32),
        compiler_params=pltpu.CompilerParams(dimension_semantics=("arbitrary",),
                                             vmem_limit_bytes=_ffn_vmem_limit(tm, d, f, fc, False)),
        name=f"ffn_ln_{n}",
    )(h, wg, wu, wd, g, b)


def _proj_ffn_ln(a, h, wo, g1, b1, wg, wu, wd, g2, b2, *, tm, fc):
    n, d = h.shape
    f = wg.shape[1]
    row = pl.BlockSpec((tm, d), lambda i: (i, 0))
    return pl.pallas_call(
        functools.partial(_proj_ffn_ln_kernel, fc=fc),
        grid=(n // tm,),
        in_specs=[row, row, _const_spec((d, d), True), _const_spec((1, d)), _const_spec((1, d)),
                  _const_spec((d, f), True), _const_spec((d, f), True), _const_spec((f, d), True),
                  _const_spec((1, d)), _const_spec((1, d))],
        out_specs=row,
        out_shape=jax.ShapeDtypeStruct((n, d), F32),
        compiler_params=pltpu.CompilerParams(dimension_semantics=("arbitrary",),
                                             vmem_limit_bytes=_ffn_vmem_limit(tm, d, f, fc, True)),
        name="proj_ffn_ln",
    )(a, h, wo, g1, b1, wg, wu, wd, g2, b2)


BIAS_PIECES = 3
VT_ROWS = 80


def _bf16_split(x):
    hi = x.astype(BF16)
    r1 = x - hi.astype(F32)
    mid = r1.astype(BF16)
    lo = (r1 - mid.astype(F32)).astype(BF16)
    return hi, mid, lo


def _bias_constants(d):
    place = np.zeros((LANES, d), np.float32)
    lane_rows = np.zeros((8, d), np.float32)
    for h in range(N_HEADS):
        base = (h // 2) * LANES + (HEAD_DIM if h % 2 == 0 else 0)
        for piece in range(BIAS_PIECES):
            neg, pos = base + piece, base + BIAS_PIECES + piece
            place[piece * N_HEADS + h, [neg, pos]] = 1.0
            lane_rows[0, pos] = 1.0
            lane_rows[1, neg] = 1.0
            lane_rows[2, neg] = -1.0
            lane_rows[3, pos] = 1.0
    lane_rows[4] = (np.arange(d) % LANES) < HEAD_DIM
    return jnp.asarray(place, BF16), jnp.asarray(lane_rows, F32)


def _inproj_kernel(h_ref, wf_ref, wq_ref, wk_ref, wvt_ref, bf_ref, c0_ref, place_ref, lanes_ref,
                   qe_ref, qo_ref, ke_ref, ko_ref, vt_ref, c_ref, carry_ref, *, tm):
    @pl.when(pl.program_id(1) == 0)
    def _():
        carry_ref[...] = c0_ref[...]

    hb = h_ref[...].astype(BF16)
    z = jnp.dot(hb, wf_ref[...], preferred_element_type=F32) + bf_ref[...]

    vt = lax.dot_general(wvt_ref[...], hb, (((1,), (1,)), ((), ())), preferred_element_type=F32)
    extra = jnp.where(lax.broadcasted_iota(jnp.int32, (VT_ROWS - HEAD_DIM, tm), 0) == 0, 1.0, 0.0)
    rows = []
    for h in range(N_HEADS):
        rows += [vt[h * HEAD_DIM:(h + 1) * HEAD_DIM], extra]
    vt_ref[...] = jnp.concatenate(rows, axis=0).astype(BF16)

    logf = (jnp.minimum(z, 0.0) - jnp.log1p(jnp.exp(-jnp.abs(z)))) * LOG2E
    tri = (lax.broadcasted_iota(jnp.int32, (tm, tm), 0)
           >= lax.broadcasted_iota(jnp.int32, (tm, tm), 1)).astype(BF16)
    cum = sum(jnp.dot(tri, piece, preferred_element_type=F32) for piece in _bf16_split(logf))
    c = carry_ref[...] + cum
    c_ref[...] = c
    carry_ref[...] = c[tm - 1:tm, :]

    q = jnp.dot(hb, wq_ref[...], preferred_element_type=F32) * (HEAD_DIM ** -0.5 * LOG2E)

    head_lane = lax.broadcasted_iota(jnp.int32, (tm, LANES), 1) < N_HEADS
    hi, mid, lo = (jnp.where(head_lane, piece.astype(F32), 0.0) for piece in _bf16_split(c))
    packed = hi + pltpu.roll(mid, N_HEADS, 1) + pltpu.roll(lo, 2 * N_HEADS, 1)
    placed = jnp.dot(packed.astype(BF16), place_ref[...], preferred_element_type=F32)
    bias_q = placed * lanes_ref[0:1, :] + lanes_ref[1:2, :]
    bias_k = placed * lanes_ref[2:3, :] + lanes_ref[3:4, :]
    first = lanes_ref[4:5, :] > 0.5

    qe_ref[...] = jnp.where(first, q, bias_q).astype(BF16)
    qo_ref[...] = jnp.where(first, bias_q, q).astype(BF16)
    k = jnp.dot(hb, wk_ref[...], preferred_element_type=F32)
    ke_ref[...] = jnp.where(first, k, bias_k).astype(BF16)
    ko_ref[...] = jnp.where(first, bias_k, k).astype(BF16)


def _inproj(h, wf, wq, wk, wvt, bfp, c0, place, lane_rows, *, tm):
    bsz, s, d = h.shape
    row = pl.BlockSpec((None, tm, d), lambda bi, j: (bi, j, 0))
    act = jax.ShapeDtypeStruct((bsz, s, d), BF16)
    wspec = _const_spec((d, d), True)
    return pl.pallas_call(
        functools.partial(_inproj_kernel, tm=tm),
        grid=(bsz, s // tm),
        in_specs=[row, _const_spec((d, LANES), True), wspec, wspec, wspec,
                  _const_spec((1, LANES)), _const_spec((1, LANES)), _const_spec((LANES, d)),
                  _const_spec((8, d))],
        out_specs=[row, row, row, row,
                   pl.BlockSpec((None, N_HEADS * VT_ROWS, tm), lambda bi, j: (bi, 0, j)),
                   pl.BlockSpec((None, tm, LANES), lambda bi, j: (bi, j, 0))],
        out_shape=[act, act, act, act,
                   jax.ShapeDtypeStruct((bsz, N_HEADS * VT_ROWS, s), BF16),
                   jax.ShapeDtypeStruct((bsz, s, LANES), F32)],
        scratch_shapes=[pltpu.VMEM((1, LANES), F32)],
        compiler_params=pltpu.CompilerParams(dimension_semantics=("arbitrary", "arbitrary"),
                                             vmem_limit_bytes=56 << 20),
        name=f"inproj_{bsz * s}",
    )(h, wf, wq, wk, wvt, bfp, c0, place, lane_rows)


def _attn_kernel(qe_ref, qo_ref, ke_ref, ko_ref, vt_ref, kme_ref, kmo_ref, vtm_ref, o_ref,
                 s_sc, mb_sc, p_sc, pm_sc, al_sc, m_sc, acc_sc, *, tq):
    nq = o_ref.shape[0] // tq
    nh = acc_sc.shape[0]

    def rows(i):
        return pl.ds(pl.multiple_of(i * tq, tq), tq)

    def slab(g):
        return slice((g // 2) * LANES, (g // 2 + 1) * LANES)

    def q_tile(g, jq):
        return (qe_ref, qo_ref)[g % 2][rows(jq), slab(g)]

    def k_chunk(g, kc):
        return (ke_ref, ko_ref)[g % 2][rows(kc), slab(g)]

    def k_meta(g):
        return (kme_ref, kmo_ref)[g % 2][:, slab(g)]

    def vt_rows(ref, g, cols):
        return ref[g * VT_ROWS:(g + 1) * VT_ROWS, cols]

    def dot_nt(a, b):
        return lax.dot_general(a, b, (((1,), (1,)), ((), ())), preferred_element_type=F32)

    causal_t = (lax.broadcasted_iota(jnp.int32, (tq, tq), 0)
                <= lax.broadcasted_iota(jnp.int32, (tq, tq), 1))
    is_meta = lax.broadcasted_iota(jnp.int32, (META_ROWS, tq), 0) < N_META

    def stage_a(jq, kc):
        for g in range(nh):
            st = dot_nt(k_chunk(g, kc), q_tile(g, jq))
            s_sc[g] = st
            mb_sc[g] = jnp.max(st, axis=0, keepdims=True)

    def stage_b(jq, first, diag):
        for g in range(nh):
            st = s_sc[g]
            if diag:
                st = jnp.where(causal_t, st, MASK_VALUE)
                sm = jnp.where(is_meta, dot_nt(k_meta(g), q_tile(g, jq)), MASK_VALUE)
                m_blk = jnp.maximum(jnp.max(st, axis=0, keepdims=True), jnp.max(sm, axis=0, keepdims=True))
            else:
                m_blk = mb_sc[g]
            if first:
                m_new = m_blk
                al_sc[g] = jnp.zeros((1, tq), F32)
            else:
                m_old = m_sc[g]
                m_new = jnp.maximum(m_old, m_blk)
                al_sc[g] = jnp.exp2(m_old - m_new)
            m_sc[g] = m_new
            p_sc[g] = jnp.exp2(st - m_new).astype(BF16)
            if diag:
                pm_sc[g] = jnp.exp2(sm - m_new).astype(BF16)

    def stage_c(jq, kc, diag):
        outs = []
        for g in range(nh):
            pv = jnp.dot(vt_rows(vt_ref, g, rows(kc)), p_sc[g], preferred_element_type=F32)
            if diag:
                pv = pv + jnp.dot(vt_rows(vtm_ref, g, slice(None)), pm_sc[g], preferred_element_type=F32)
            acc = al_sc[g] * acc_sc[g] + pv
            acc_sc[g] = acc
            if diag:
                outs.append(acc[:HEAD_DIM] / acc[HEAD_DIM:HEAD_DIM + 1])
        for g in range(0, len(outs), 2):
            o_ref[rows(jq), slab(g)] = jnp.concatenate(outs[g:g + 2], axis=0).T.astype(BF16)

    acc_sc[...] = jnp.zeros(acc_sc.shape, F32)
    stage_a(0, 0)
    stage_b(0, True, True)
    stage_a(1, 0)

    def tile(jq, carry):
        stage_c(jq - 1, jq - 1, True)
        stage_b(jq, True, False)
        stage_a(jq, 1)

        def chunk(kc, c):
            stage_c(jq, kc - 1, False)
            stage_b(jq, False, False)
            stage_a(jq, kc + 1)
            return c

        lax.fori_loop(1, jq, chunk, 0)
        stage_c(jq, jq - 1, False)
        stage_b(jq, False, True)
        stage_a(jnp.minimum(jq + 1, nq - 1), 0)
        return carry

    lax.fori_loop(1, nq, tile, 0)
    stage_c(nq - 1, nq - 1, True)


def _attention(qe, qo, ke, ko, vt, kme, kmo, vtm, *, tq, npair):
    bsz, s, d = ke.shape
    nh = 2 * npair
    width = npair * LANES
    seq = pl.BlockSpec((None, s, width), lambda bi, hp: (bi, 0, hp))
    kmspec = pl.BlockSpec((META_ROWS, width), lambda bi, hp: (0, hp))
    return pl.pallas_call(
        functools.partial(_attn_kernel, tq=tq),
        grid=(bsz, d // width),
        in_specs=[seq, seq, seq, seq,
                  pl.BlockSpec((None, nh * VT_ROWS, s), lambda bi, hp: (bi, hp, 0)),
                  kmspec, kmspec,
                  pl.BlockSpec((nh * VT_ROWS, META_ROWS), lambda bi, hp: (hp, 0))],
        out_specs=seq,
        out_shape=jax.ShapeDtypeStruct((bsz, s, d), BF16),
        scratch_shapes=[pltpu.VMEM((nh, tq, tq), F32), pltpu.VMEM((nh, 1, tq), F32),
                        pltpu.VMEM((nh, tq, tq), BF16), pltpu.VMEM((nh, META_ROWS, tq), BF16),
                        pltpu.VMEM((nh, 1, tq), F32), pltpu.VMEM((nh, 1, tq), F32),
                        pltpu.VMEM((nh, VT_ROWS, tq), F32)],
        compiler_params=pltpu.CompilerParams(dimension_semantics=("arbitrary", "arbitrary"),
                                             vmem_limit_bytes=56 << 20),
        name="fox_attn",
    )(qe, qo, ke, ko, vt, kme, kmo, vtm)


def kernel(x, meta_tokens, pool_w, pool_scale, fox_w_in, fox_b_f, fox_w_o, ffn_w_gate, ffn_w_up,
           ffn_w_down, ln_g, ln_b):
    bsz, s, d = x.shape
    f_dim = ffn_w_gate.shape[-1]
    tm = 512
    tq = 512
    fc = -(-f_dim // (2 * MXU_TILE)) * MXU_TILE
    row = lambda a: a.reshape(1, -1).astype(F32)

    meta_pad = jnp.pad(meta_tokens, ((0, META_ROWS - N_META), (0, 0)))
    wg = ffn_w_gate.astype(BF16)
    wu = ffn_w_up.astype(BF16)
    wd = ffn_w_down.astype(BF16)

    h1x, h1m = _pool_ln(x, meta_tokens, meta_pad, pool_w[0].astype(BF16), row(pool_scale[0]),
                        row(ln_g[0, 0]), row(ln_b[0, 0]), tm=tm)
    ffn0 = functools.partial(_ffn_ln, wg=wg[0], wu=wu[0], wd=wd[0], g=row(ln_g[0, 1]), b=row(ln_b[0, 1]), fc=fc)
    h2x = ffn0(h1x.reshape(bsz * s, d), tm=tm)
    h2m = ffn0(h1m, tm=META_ROWS)

    w_in = fox_w_in[0]
    wq, wk = (w_in[:, i * d:(i + 1) * d].astype(BF16) for i in range(2))
    wvt = w_in[:, 2 * d:3 * d].T.astype(BF16)
    wf = jnp.pad(w_in[:, 3 * d:], ((0, 0), (0, LANES - N_HEADS))).astype(BF16)
    bfp = jnp.pad(fox_b_f[0], (0, LANES - N_HEADS)).reshape(1, LANES)
    place, lane_rows = _bias_constants(d)
    inproj = functools.partial(_inproj, wf=wf, wq=wq, wk=wk, wvt=wvt, bfp=bfp, place=place, lane_rows=lane_rows)
    _, _, kme, kmo, vtm, cm = inproj(h2m[None], c0=jnp.zeros((1, LANES), F32), tm=META_ROWS)
    qe, qo, ke, ko, vt, _ = inproj(h2x.reshape(bsz, s, d), c0=cm[0, N_META - 1:N_META, :], tm=tm)
    o = _attention(qe, qo, ke, ko, vt, kme[0], kmo[0], vtm[0], tq=tq, npair=2)
    out = _proj_ffn_ln(o.reshape(bsz * s, d), h2x, fox_w_o[0].astype(BF16), row(ln_g[1, 0]), row(ln_b[1, 0]),
                       wg[1], wu[1], wd[1], row(ln_g[1, 1]), row(ln_b[1, 1]), tm=tm, fc=fc)
    return out.reshape(bsz, s, d)
```

```python
import functools
import math

import jax
import jax.numpy as jnp
import numpy as np
from jax import lax
from jax.experimental import pallas as pl
from jax.experimental.pallas import tpu as pltpu

N_META = 16
N_HEADS = 16
HEAD_DIM = 64
POOL_WINDOWS = (2, 4, 8, 16)
MAX_WINDOW = max(POOL_WINDOWS)
DEPTH = 2
DN_ALPHA = (2.0 * DEPTH) ** 0.25
LN_EPS = 1e-5
LANES = 128
MXU_TILE = 256
VMEM_BUDGET = 60 << 20
META_ROWS = 128
MASK_VALUE = -1e30
LOG2E = math.log2(math.e)

F32 = jnp.float32
BF16 = jnp.bfloat16


def _layer_norm(z, g, b):
    mu = jnp.mean(z, axis=-1, keepdims=True)
    zc = z - mu
    var = jnp.mean(zc * zc, axis=-1, keepdims=True)
    return zc * lax.rsqrt(var + LN_EPS) * g + b


def _const_spec(shape, single_buffer=False):
    zeros = (0,) * len(shape)
    if single_buffer:
        return pl.BlockSpec(shape, lambda *_: zeros, pipeline_mode=pl.Buffered(1))
    return pl.BlockSpec(shape, lambda *_: zeros)


def _swiglu(hb, wg_ref, wu_ref, wd_ref, fc):
    f_dim = wg_ref.shape[1]
    out = None
    for lo in range(0, f_dim, fc):
        sl = slice(lo, min(lo + fc, f_dim))
        hg = jnp.dot(hb, wg_ref[:, sl], preferred_element_type=F32)
        hu = jnp.dot(hb, wu_ref[:, sl], preferred_element_type=F32)
        a = (jax.nn.silu(hg) * hu).astype(BF16)
        part = jnp.dot(a, wd_ref[sl, :], preferred_element_type=F32)
        out = part if out is None else out + part
    return out


def _ffn_ln(h, wg_ref, wu_ref, wd_ref, g_ref, b_ref, fc):
    f = _swiglu(h.astype(BF16), wg_ref, wu_ref, wd_ref, fc)
    return _layer_norm(DN_ALPHA * h + f, g_ref[...], b_ref[...])


def _pool_mix(ext_ref, row0, rows, t1, w_ref, sc_ref):
    d = ext_ref.shape[-1]
    cgrp = d // len(POOL_WINDOWS)
    zs = []
    for g, w in enumerate(POOL_WINDOWS):
        sl = slice(g * cgrp, (g + 1) * cgrp)
        cur_g = ext_ref[pl.ds(MAX_WINDOW + row0, rows), sl]
        win = cur_g
        for i in range(1, w):
            win = win + ext_ref[pl.ds(MAX_WINDOW + row0 - i, rows), sl]
        y = win / jnp.minimum(t1, float(w)) - cur_g
        m = jnp.dot(y.astype(BF16), w_ref[g], preferred_element_type=F32) * sc_ref[:, sl]
        zs.append(DN_ALPHA * cur_g + m)
    return jnp.concatenate(zs, axis=1)


def _layer0_kernel(x_ref, halo_ref, meta_ref, pw_ref, psc_ref, g1_ref, b1_ref, wg_ref, wu_ref, wd_ref,
                   g2_ref, b2_ref, o_ref, ext_ref, *, tm, sub, is_meta, fc):
    d = x_ref.shape[-1]
    ext_ref[MAX_WINDOW:, :] = x_ref[...]
    if is_meta:
        ext_ref[:MAX_WINDOW, :] = jnp.zeros((MAX_WINDOW, d), F32)
        pos0 = 0
    else:
        j = pl.program_id(1)

        @pl.when(j == 0)
        def _():
            ext_ref[:MAX_WINDOW, :] = meta_ref[...]

        @pl.when(j > 0)
        def _():
            ext_ref[:MAX_WINDOW, :] = halo_ref[...]

        pos0 = N_META + j * tm
    def mixer(r):
        t1 = (pos0 + r + 1 + lax.broadcasted_iota(jnp.int32, (sub, 1), 0)).astype(F32)
        return _layer_norm(_pool_mix(ext_ref, r, sub, t1, pw_ref, psc_ref), g1_ref[...], b1_ref[...])

    h_next = mixer(0)
    for r in range(0, tm, sub):
        h1 = h_next
        if r + sub < tm:
            h_next = mixer(r + sub)
        o_ref[r:r + sub, :] = _ffn_ln(h1, wg_ref, wu_ref, wd_ref, g2_ref, b2_ref, fc)


def _layer1_tail_kernel(a_ref, h_ref, wo_ref, g1_ref, b1_ref, wg_ref, wu_ref, wd_ref, g2_ref, b2_ref,
                        o_ref, *, tm, sub, fc):
    def mixer(r):
        m = jnp.dot(a_ref[r:r + sub, :], wo_ref[...], preferred_element_type=F32)
        return _layer_norm(DN_ALPHA * h_ref[r:r + sub, :] + m, g1_ref[...], b1_ref[...])

    h_next = mixer(0)
    for r in range(0, tm, sub):
        h = h_next
        if r + sub < tm:
            h_next = mixer(r + sub)
        o_ref[r:r + sub, :] = _ffn_ln(h, wg_ref, wu_ref, wd_ref, g2_ref, b2_ref, fc)


def _ffn_vmem_limit(tm, sub, d, f, fc, extra):
    weights = 3 * d * f * 2
    tiles = 2 * 2 * tm * d * 4
    temps = 3 * (sub * fc * (4 + 4 + 4 + 2) + 4 * sub * d * 4)
    return min(weights + tiles + temps + extra + (6 << 20), VMEM_BUDGET)


def _layer0(x, meta_tokens, meta_pad, pw, psc, g1, b1, wg, wu, wd, g2, b2, *, tm, sub, fc):
    bsz, s, d = x.shape
    f = wg.shape[1]
    ngrp = len(POOL_WINDOWS)
    cgrp = d // ngrp
    hb = tm // MAX_WINDOW
    vec = _const_spec((1, d))
    consts = [_const_spec((ngrp, cgrp, cgrp)), vec, vec, vec,
              _const_spec((d, f), True), _const_spec((d, f), True), _const_spec((f, d), True), vec, vec]
    args = (pw, psc, g1, b1, wg, wu, wd, g2, b2)
    hx = pl.pallas_call(
        functools.partial(_layer0_kernel, tm=tm, sub=sub, is_meta=False, fc=fc),
        grid=(bsz, s // tm),
        in_specs=[pl.BlockSpec((None, tm, d), lambda bi, j: (bi, j, 0)),
                  pl.BlockSpec((None, MAX_WINDOW, d), lambda bi, j: (bi, jnp.maximum(j * hb - 1, 0), 0)),
                  _const_spec((N_META, d))] + consts,
        out_specs=pl.BlockSpec((None, tm, d), lambda bi, j: (bi, j, 0)),
        out_shape=jax.ShapeDtypeStruct((bsz, s, d), F32),
        scratch_shapes=[pltpu.VMEM((MAX_WINDOW + tm, d), F32)],
        compiler_params=pltpu.CompilerParams(
            dimension_semantics=("arbitrary", "arbitrary"),
            vmem_limit_bytes=_ffn_vmem_limit(tm, sub, d, f, fc, (MAX_WINDOW + tm) * d * 4)),
        name="layer0_x",
    )(x, x, meta_tokens, *args)
    hm = pl.pallas_call(
        functools.partial(_layer0_kernel, tm=META_ROWS, sub=META_ROWS, is_meta=True, fc=fc),
        grid=(1,),
        in_specs=[_const_spec((META_ROWS, d)), _const_spec((MAX_WINDOW, d)), _const_spec((N_META, d))] + consts,
        out_specs=_const_spec((META_ROWS, d)),
        out_shape=jax.ShapeDtypeStruct((META_ROWS, d), F32),
        scratch_shapes=[pltpu.VMEM((MAX_WINDOW + META_ROWS, d), F32)],
        compiler_params=pltpu.CompilerParams(
            vmem_limit_bytes=_ffn_vmem_limit(META_ROWS, META_ROWS, d, f, fc, 0)),
        name="layer0_meta",
    )(meta_pad, meta_tokens, meta_tokens, *args)
    return hx, hm


def _layer1_tail(a, h, wo, g1, b1, wg, wu, wd, g2, b2, *, tm, sub, fc):
    n, d = h.shape
    f = wg.shape[1]
    row = pl.BlockSpec((tm, d), lambda i: (i, 0))
    vec = _const_spec((1, d))
    return pl.pallas_call(
        functools.partial(_layer1_tail_kernel, tm=tm, sub=sub, fc=fc),
        grid=(n // tm,),
        in_specs=[row, row, _const_spec((d, d), True), vec, vec,
                  _const_spec((d, f), True), _const_spec((d, f), True), _const_spec((f, d), True), vec, vec],
        out_specs=row,
        out_shape=jax.ShapeDtypeStruct((n, d), F32),
        compiler_params=pltpu.CompilerParams(
            dimension_semantics=("arbitrary",),
            vmem_limit_bytes=_ffn_vmem_limit(tm, sub, d, f, fc, d * d * 2 + 2 * tm * d * 2 + 2 * tm * d * 4)),
        name="layer1_tail",
    )(a, h, wo, g1, b1, wg, wu, wd, g2, b2)


BIAS_PIECES = 3
VT_ROWS = 80


def _bf16_split(x):
    hi = x.astype(BF16)
    r1 = x - hi.astype(F32)
    mid = r1.astype(BF16)
    lo = (r1 - mid.astype(F32)).astype(BF16)
    return hi, mid, lo


def _bias_constants(d):
    place = np.zeros((LANES, d), np.float32)
    lane_rows = np.zeros((8, d), np.float32)
    for h in range(N_HEADS):
        base = (h // 2) * LANES + (HEAD_DIM if h % 2 == 0 else 0)
        for piece in range(BIAS_PIECES):
            neg, pos = base + piece, base + BIAS_PIECES + piece
            place[piece * N_HEADS + h, [neg, pos]] = 1.0
            lane_rows[0, pos] = 1.0
            lane_rows[1, neg] = 1.0
            lane_rows[2, neg] = -1.0
            lane_rows[3, pos] = 1.0
    lane_rows[4] = (np.arange(d) % LANES) < HEAD_DIM
    return jnp.asarray(place, BF16), jnp.asarray(lane_rows, F32)


def _inproj_kernel(h_ref, wf_ref, wq_ref, wk_ref, wvt_ref, bf_ref, c0_ref, place_ref, lanes_ref,
                   qe_ref, qo_ref, ke_ref, ko_ref, vt_ref, c_ref, carry_ref, *, tm):
    @pl.when(pl.program_id(1) == 0)
    def _():
        carry_ref[...] = c0_ref[...]

    hb = h_ref[...].astype(BF16)
    z = jnp.dot(hb, wf_ref[...], preferred_element_type=F32) + bf_ref[...]

    vt = lax.dot_general(wvt_ref[...], hb, (((1,), (1,)), ((), ())), preferred_element_type=F32)
    extra = jnp.where(lax.broadcasted_iota(jnp.int32, (VT_ROWS - HEAD_DIM, tm), 0) == 0, 1.0, 0.0)
    rows = []
    for h in range(N_HEADS):
        rows += [vt[h * HEAD_DIM:(h + 1) * HEAD_DIM], extra]
    vt_ref[...] = jnp.concatenate(rows, axis=0).astype(BF16)

    logf = (jnp.minimum(z, 0.0) - jnp.log1p(jnp.exp(-jnp.abs(z)))) * LOG2E
    tri = (lax.broadcasted_iota(jnp.int32, (tm, tm), 0)
           >= lax.broadcasted_iota(jnp.int32, (tm, tm), 1)).astype(BF16)
    cum = sum(jnp.dot(tri, piece, preferred_element_type=F32) for piece in _bf16_split(logf))
    c = carry_ref[...] + cum
    c_ref[...] = c
    carry_ref[...] = c[tm - 1:tm, :]

    q = jnp.dot(hb, wq_ref[...], preferred_element_type=F32) * (HEAD_DIM ** -0.5 * LOG2E)

    head_lane = lax.broadcasted_iota(jnp.int32, (tm, LANES), 1) < N_HEADS
    hi, mid, lo = (jnp.where(head_lane, piece.astype(F32), 0.0) for piece in _bf16_split(c))
    packed = hi + pltpu.roll(mid, N_HEADS, 1) + pltpu.roll(lo, 2 * N_HEADS, 1)
    placed = jnp.dot(packed.astype(BF16), place_ref[...], preferred_element_type=F32)
    bias_q = placed * lanes_ref[0:1, :] + lanes_ref[1:2, :]
    bias_k = placed * lanes_ref[2:3, :] + lanes_ref[3:4, :]
    first = lanes_ref[4:5, :] > 0.5

    qe_ref[...] = jnp.where(first, q, bias_q).astype(BF16)
    qo_ref[...] = jnp.where(first, bias_q, q).astype(BF16)
    k = jnp.dot(hb, wk_ref[...], preferred_element_type=F32)
    ke_ref[...] = jnp.where(first, k, bias_k).astype(BF16)
    ko_ref[...] = jnp.where(first, bias_k, k).astype(BF16)


def _inproj(h, wf, wq, wk, wvt, bfp, c0, place, lane_rows, *, tm):
    bsz, s, d = h.shape
    row = pl.BlockSpec((None, tm, d), lambda bi, j: (bi, j, 0))
    act = jax.ShapeDtypeStruct((bsz, s, d), BF16)
    wspec = _const_spec((d, d), True)
    return pl.pallas_call(
        functools.partial(_inproj_kernel, tm=tm),
        grid=(bsz, s // tm),
        in_specs=[row, _const_spec((d, LANES), True), wspec, wspec, wspec,
                  _const_spec((1, LANES)), _const_spec((1, LANES)), _const_spec((LANES, d)),
                  _const_spec((8, d))],
        out_specs=[row, row, row, row,
                   pl.BlockSpec((None, N_HEADS * VT_ROWS, tm), lambda bi, j: (bi, 0, j)),
                   pl.BlockSpec((None, tm, LANES), lambda bi, j: (bi, j, 0))],
        out_shape=[act, act, act, act,
                   jax.ShapeDtypeStruct((bsz, N_HEADS * VT_ROWS, s), BF16),
                   jax.ShapeDtypeStruct((bsz, s, LANES), F32)],
        scratch_shapes=[pltpu.VMEM((1, LANES), F32)],
        compiler_params=pltpu.CompilerParams(dimension_semantics=("arbitrary", "arbitrary"),
                                             vmem_limit_bytes=56 << 20),
        name=f"inproj_{bsz * s}",
    )(h, wf, wq, wk, wvt, bfp, c0, place, lane_rows)


def _attn_kernel(qe_ref, qo_ref, ke_ref, ko_ref, vt_ref, kme_ref, kmo_ref, vtm_ref, o_ref,
                 s_sc, mb_sc, p_sc, pm_sc, al_sc, m_sc, acc_sc, *, tq):
    nq = o_ref.shape[0] // tq
    nh = acc_sc.shape[0]

    def rows(i):
        return pl.ds(pl.multiple_of(i * tq, tq), tq)

    def slab(g):
        return slice((g // 2) * LANES, (g // 2 + 1) * LANES)

    def q_tile(g, jq):
        return (qe_ref, qo_ref)[g % 2][rows(jq), slab(g)]

    def k_chunk(g, kc):
        return (ke_ref, ko_ref)[g % 2][rows(kc), slab(g)]

    def k_meta(g):
        return (kme_ref, kmo_ref)[g % 2][:, slab(g)]

    def vt_rows(ref, g, cols):
        return ref[g * VT_ROWS:(g + 1) * VT_ROWS, cols]

    def dot_nt(a, b):
        return lax.dot_general(a, b, (((1,), (1,)), ((), ())), preferred_element_type=F32)

    causal_t = (lax.broadcasted_iota(jnp.int32, (tq, tq), 0)
                <= lax.broadcasted_iota(jnp.int32, (tq, tq), 1))
    is_meta = lax.broadcasted_iota(jnp.int32, (META_ROWS, tq), 0) < N_META

    def stage_a(jq, kc):
        for g in range(nh):
            st = dot_nt(k_chunk(g, kc), q_tile(g, jq))
            s_sc[g] = st
            mb_sc[g] = jnp.max(st, axis=0, keepdims=True)

    def stage_b(jq, first, diag):
        for g in range(nh):
            st = s_sc[g]
            if diag:
                st = jnp.where(causal_t, st, MASK_VALUE)
                sm = jnp.where(is_meta, dot_nt(k_meta(g), q_tile(g, jq)), MASK_VALUE)
                m_blk = jnp.maximum(jnp.max(st, axis=0, keepdims=True), jnp.max(sm, axis=0, keepdims=True))
            else:
                m_blk = mb_sc[g]
            if first:
                m_new = m_blk
                al_sc[g] = jnp.zeros((1, tq), F32)
            else:
                m_old = m_sc[g]
                m_new = jnp.maximum(m_old, m_blk)
                al_sc[g] = jnp.exp2(m_old - m_new)
            m_sc[g] = m_new
            p_sc[g] = jnp.exp2(st - m_new).astype(BF16)
            if diag:
                pm_sc[g] = jnp.exp2(sm - m_new).astype(BF16)

    def stage_c(jq, kc, diag):
        outs = []
        for g in range(nh):
            pv = jnp.dot(vt_rows(vt_ref, g, rows(kc)), p_sc[g], preferred_element_type=F32)
            if diag:
                pv = pv + jnp.dot(vt_rows(vtm_ref, g, slice(None)), pm_sc[g], preferred_element_type=F32)
            acc = al_sc[g] * acc_sc[g] + pv
            acc_sc[g] = acc
            if diag:
                outs.append(acc[:HEAD_DIM] / acc[HEAD_DIM:HEAD_DIM + 1])
        for g in range(0, len(outs), 2):
            o_ref[rows(jq), slab(g)] = jnp.concatenate(outs[g:g + 2], axis=0).T.astype(BF16)

    acc_sc[...] = jnp.zeros(acc_sc.shape, F32)
    stage_a(0, 0)
    stage_b(0, True, True)
    stage_a(1, 0)

    def tile(jq, carry):
        stage_c(jq - 1, jq - 1, True)
        stage_b(jq, True, False)
        stage_a(jq, 1)

        def chunk(kc, c):
            stage_c(jq, kc - 1, False)
            stage_b(jq, False, False)
            stage_a(jq, kc + 1)
            return c

        lax.fori_loop(1, jq, chunk, 0)
        stage_c(jq, jq - 1, False)
        stage_b(jq, False, True)
        stage_a(jnp.minimum(jq + 1, nq - 1), 0)
        return carry

    lax.fori_loop(1, nq, tile, 0)
    stage_c(nq - 1, nq - 1, True)


def _attention(qe, qo, ke, ko, vt, kme, kmo, vtm, *, tq, npair):
    bsz, s, d = ke.shape
    nh = 2 * npair
    width = npair * LANES
    seq = pl.BlockSpec((None, s, width), lambda bi, hp: (bi, 0, hp))
    kmspec = pl.BlockSpec((META_ROWS, width), lambda bi, hp: (0, hp))
    return pl.pallas_call(
        functools.partial(_attn_kernel, tq=tq),
        grid=(bsz, d // width),
        in_specs=[seq, seq, seq, seq,
                  pl.BlockSpec((None, nh * VT_ROWS, s), lambda bi, hp: (bi, hp, 0)),
                  kmspec, kmspec,
                  pl.BlockSpec((nh * VT_ROWS, META_ROWS), lambda bi, hp: (hp, 0))],
        out_specs=seq,
        out_shape=jax.ShapeDtypeStruct((bsz, s, d), BF16),
        scratch_shapes=[pltpu.VMEM((nh, tq, tq), F32), pltpu.VMEM((nh, 1, tq), F32),
                        pltpu.VMEM((nh, tq, tq), BF16), pltpu.VMEM((nh, META_ROWS, tq), BF16),
                        pltpu.VMEM((nh, 1, tq), F32), pltpu.VMEM((nh, 1, tq), F32),
                        pltpu.VMEM((nh, VT_ROWS, tq), F32)],
        compiler_params=pltpu.CompilerParams(dimension_semantics=("arbitrary", "arbitrary"),
                                             vmem_limit_bytes=56 << 20),
        name="fox_attn",
    )(qe, qo, ke, ko, vt, kme, kmo, vtm)


def kernel(x, meta_tokens, pool_w, pool_scale, fox_w_in, fox_b_f, fox_w_o, ffn_w_gate, ffn_w_up,
           ffn_w_down, ln_g, ln_b):
    bsz, s, d = x.shape
    f_dim = ffn_w_gate.shape[-1]
    tm = 1024
    sub = 256
    tq = 512
    fc = -(-f_dim // (2 * MXU_TILE)) * MXU_TILE
    row = lambda a: a.reshape(1, -1).astype(F32)

    meta_pad = jnp.pad(meta_tokens, ((0, META_ROWS - N_META), (0, 0)))
    wg = ffn_w_gate.astype(BF16)
    wu = ffn_w_up.astype(BF16)
    wd = ffn_w_down.astype(BF16)

    h2x, h2m = _layer0(x, meta_tokens, meta_pad, pool_w[0].astype(BF16), row(pool_scale[0]),
                       row(ln_g[0, 0]), row(ln_b[0, 0]), wg[0], wu[0], wd[0], row(ln_g[0, 1]), row(ln_b[0, 1]),
                       tm=tm, sub=sub, fc=fc)
    h2x = h2x.reshape(bsz * s, d)

    w_in = fox_w_in[0]
    wq, wk = (w_in[:, i * d:(i + 1) * d].astype(BF16) for i in range(2))
    wvt = w_in[:, 2 * d:3 * d].T.astype(BF16)
    wf = jnp.pad(w_in[:, 3 * d:], ((0, 0), (0, LANES - N_HEADS))).astype(BF16)
    bfp = jnp.pad(fox_b_f[0], (0, LANES - N_HEADS)).reshape(1, LANES)
    place, lane_rows = _bias_constants(d)
    inproj = functools.partial(_inproj, wf=wf, wq=wq, wk=wk, wvt=wvt, bfp=bfp, place=place, lane_rows=lane_rows)
    _, _, kme, kmo, vtm, cm = inproj(h2m[None], c0=jnp.zeros((1, LANES), F32), tm=META_ROWS)
    qe, qo, ke, ko, vt, _ = inproj(h2x.reshape(bsz, s, d), c0=cm[0, N_META - 1:N_META, :], tm=512)
    o = _attention(qe, qo, ke, ko, vt, kme[0], kmo[0], vtm[0], tq=tq, npair=2)
    out = _layer1_tail(o.reshape(bsz * s, d), h2x, fox_w_o[0].astype(BF16), row(ln_g[1, 0]), row(ln_b[1, 0]),
                       wg[1], wu[1], wd[1], row(ln_g[1, 1]), row(ln_b[1, 1]), tm=tm, sub=sub, fc=fc)
    return out.reshape(bsz, s, d)
```

```python
import functools
import math

import jax
import jax.numpy as jnp
import numpy as np
from jax import lax
from jax.experimental import pallas as pl
from jax.experimental.pallas import tpu as pltpu

N_META = 16
N_HEADS = 16
HEAD_DIM = 64
POOL_WINDOWS = (2, 4, 8, 16)
MAX_WINDOW = max(POOL_WINDOWS)
DEPTH = 2
DN_ALPHA = (2.0 * DEPTH) ** 0.25
LN_EPS = 1e-5
LANES = 128
MXU_TILE = 256
VMEM_BUDGET = 60 << 20
META_ROWS = 128
MASK_VALUE = -1e30
LOG2E = math.log2(math.e)

F32 = jnp.float32
BF16 = jnp.bfloat16


def _layer_norm(z, g, b):
    mu = jnp.mean(z, axis=-1, keepdims=True)
    zc = z - mu
    var = jnp.mean(zc * zc, axis=-1, keepdims=True)
    return zc * lax.rsqrt(var + LN_EPS) * g + b


def _const_spec(shape, single_buffer=False):
    zeros = (0,) * len(shape)
    if single_buffer:
        return pl.BlockSpec(shape, lambda *_: zeros, pipeline_mode=pl.Buffered(1))
    return pl.BlockSpec(shape, lambda *_: zeros)


def _swiglu(hb, wg_ref, wu_ref, wd_ref, fc):
    f_dim = wg_ref.shape[1]
    out = None
    for lo in range(0, f_dim, fc):
        sl = slice(lo, min(lo + fc, f_dim))
        hg = jnp.dot(hb, wg_ref[:, sl], preferred_element_type=F32)
        hu = jnp.dot(hb, wu_ref[:, sl], preferred_element_type=F32)
        a = (jax.nn.silu(hg) * hu).astype(BF16)
        part = jnp.dot(a, wd_ref[sl, :], preferred_element_type=F32)
        out = part if out is None else out + part
    return out


def _ffn_ln(h, wg_ref, wu_ref, wd_ref, g_ref, b_ref, fc):
    f = _swiglu(h.astype(BF16), wg_ref, wu_ref, wd_ref, fc)
    return _layer_norm(DN_ALPHA * h + f, g_ref[...], b_ref[...])


def _pool_mix(ext_ref, row0, rows, t1, w_ref, sc_ref):
    d = ext_ref.shape[-1]
    cgrp = d // len(POOL_WINDOWS)
    zs = []
    for g, w in enumerate(POOL_WINDOWS):
        sl = slice(g * cgrp, (g + 1) * cgrp)
        cur_g = ext_ref[pl.ds(MAX_WINDOW + row0, rows), sl]
        win = cur_g
        for i in range(1, w):
            win = win + ext_ref[pl.ds(MAX_WINDOW + row0 - i, rows), sl]
        y = win / jnp.minimum(t1, float(w)) - cur_g
        m = jnp.dot(y.astype(BF16), w_ref[g], preferred_element_type=F32) * sc_ref[:, sl]
        zs.append(DN_ALPHA * cur_g + m)
    return jnp.concatenate(zs, axis=1)


def _layer0_kernel(x_ref, halo_ref, meta_ref, pw_ref, psc_ref, g1_ref, b1_ref, wg_ref, wu_ref, wd_ref,
                   g2_ref, b2_ref, o_ref, ext_ref, *, tm, sub, is_meta, fc):
    d = x_ref.shape[-1]
    ext_ref[MAX_WINDOW:, :] = x_ref[...]
    if is_meta:
        ext_ref[:MAX_WINDOW, :] = jnp.zeros((MAX_WINDOW, d), F32)
        pos0 = 0
    else:
        j = pl.program_id(1)

        @pl.when(j == 0)
        def _():
            ext_ref[:MAX_WINDOW, :] = meta_ref[...]

        @pl.when(j > 0)
        def _():
            ext_ref[:MAX_WINDOW, :] = halo_ref[...]

        pos0 = N_META + j * tm
    def mixer(r):
        t1 = (pos0 + r + 1 + lax.broadcasted_iota(jnp.int32, (sub, 1), 0)).astype(F32)
        return _layer_norm(_pool_mix(ext_ref, r, sub, t1, pw_ref, psc_ref), g1_ref[...], b1_ref[...])

    h_next = mixer(0)
    for r in range(0, tm, sub):
        h1 = h_next
        if r + sub < tm:
            h_next = mixer(r + sub)
        o_ref[r:r + sub, :] = _ffn_ln(h1, wg_ref, wu_ref, wd_ref, g2_ref, b2_ref, fc)


def _layer1_tail_kernel(a_ref, h_ref, wo_ref, g1_ref, b1_ref, wg_ref, wu_ref, wd_ref, g2_ref, b2_ref,
                        o_ref, *, tm, sub, fc):
    def mixer(r):
        m = jnp.dot(a_ref[r:r + sub, :], wo_ref[...], preferred_element_type=F32)
        return _layer_norm(DN_ALPHA * h_ref[r:r + sub, :] + m, g1_ref[...], b1_ref[...])

    h_next = mixer(0)
    for r in range(0, tm, sub):
        h = h_next
        if r + sub < tm:
            h_next = mixer(r + sub)
        o_ref[r:r + sub, :] = _ffn_ln(h, wg_ref, wu_ref, wd_ref, g2_ref, b2_ref, fc)


def _ffn_vmem_limit(tm, sub, d, f, fc, extra):
    weights = 3 * d * f * 2
    tiles = 2 * 2 * tm * d * 4
    temps = 3 * (sub * fc * (4 + 4 + 4 + 2) + 4 * sub * d * 4)
    return min(weights + tiles + temps + extra + (6 << 20), VMEM_BUDGET)


def _layer0(x, meta_tokens, meta_pad, pw, psc, g1, b1, wg, wu, wd, g2, b2, *, tm, sub, fc):
    bsz, s, d = x.shape
    f = wg.shape[1]
    ngrp = len(POOL_WINDOWS)
    cgrp = d // ngrp
    hb = tm // MAX_WINDOW
    vec = _const_spec((1, d))
    consts = [_const_spec((ngrp, cgrp, cgrp)), vec, vec, vec,
              _const_spec((d, f), True), _const_spec((d, f), True), _const_spec((f, d), True), vec, vec]
    args = (pw, psc, g1, b1, wg, wu, wd, g2, b2)
    hx = pl.pallas_call(
        functools.partial(_layer0_kernel, tm=tm, sub=sub, is_meta=False, fc=fc),
        grid=(bsz, s // tm),
        in_specs=[pl.BlockSpec((None, tm, d), lambda bi, j: (bi, j, 0)),
                  pl.BlockSpec((None, MAX_WINDOW, d), lambda bi, j: (bi, jnp.maximum(j * hb - 1, 0), 0)),
                  _const_spec((N_META, d))] + consts,
        out_specs=pl.BlockSpec((None, tm, d), lambda bi, j: (bi, j, 0)),
        out_shape=jax.ShapeDtypeStruct((bsz, s, d), F32),
        scratch_shapes=[pltpu.VMEM((MAX_WINDOW + tm, d), F32)],
        compiler_params=pltpu.CompilerParams(
            dimension_semantics=("arbitrary", "arbitrary"),
            vmem_limit_bytes=_ffn_vmem_limit(tm, sub, d, f, fc, (MAX_WINDOW + tm) * d * 4)),
        name="layer0_x",
    )(x, x, meta_tokens, *args)
    hm = pl.pallas_call(
        functools.partial(_layer0_kernel, tm=META_ROWS, sub=META_ROWS, is_meta=True, fc=fc),
        grid=(1,),
        in_specs=[_const_spec((META_ROWS, d)), _const_spec((MAX_WINDOW, d)), _const_spec((N_META, d))] + consts,
        out_specs=_const_spec((META_ROWS, d)),
        out_shape=jax.ShapeDtypeStruct((META_ROWS, d), F32),
        scratch_shapes=[pltpu.VMEM((MAX_WINDOW + META_ROWS, d), F32)],
        compiler_params=pltpu.CompilerParams(
            vmem_limit_bytes=_ffn_vmem_limit(META_ROWS, META_ROWS, d, f, fc, 0)),
        name="layer0_meta",
    )(meta_pad, meta_tokens, meta_tokens, *args)
    return hx, hm


def _layer1_tail(a, h, wo, g1, b1, wg, wu, wd, g2, b2, *, tm, sub, fc):
    n, d = h.shape
    f = wg.shape[1]
    row = pl.BlockSpec((tm, d), lambda i: (i, 0))
    vec = _const_spec((1, d))
    return pl.pallas_call(
        functools.partial(_layer1_tail_kernel, tm=tm, sub=sub, fc=fc),
        grid=(n // tm,),
        in_specs=[row, row, _const_spec((d, d), True), vec, vec,
                  _const_spec((d, f), True), _const_spec((d, f), True), _const_spec((f, d), True), vec, vec],
        out_specs=row,
        out_shape=jax.ShapeDtypeStruct((n, d), F32),
        compiler_params=pltpu.CompilerParams(
            dimension_semantics=("arbitrary",),
            vmem_limit_bytes=_ffn_vmem_limit(tm, sub, d, f, fc, d * d * 2 + 2 * tm * d * 2 + 2 * tm * d * 4)),
        name="layer1_tail",
    )(a, h, wo, g1, b1, wg, wu, wd, g2, b2)


BIAS_PIECES = 3
VT_ROWS = 80


def _bf16_split(x):
    hi = x.astype(BF16)
    r1 = x - hi.astype(F32)
    mid = r1.astype(BF16)
    lo = (r1 - mid.astype(F32)).astype(BF16)
    return hi, mid, lo


def _bias_constants(d):
    place = np.zeros((LANES, d), np.float32)
    lane_rows = np.zeros((8, d), np.float32)
    for h in range(N_HEADS):
        base = (h // 2) * LANES + (HEAD_DIM if h % 2 == 0 else 0)
        for piece in range(BIAS_PIECES):
            neg, pos = base + piece, base + BIAS_PIECES + piece
            place[piece * N_HEADS + h, [neg, pos]] = 1.0
            lane_rows[0, pos] = 1.0
            lane_rows[1, neg] = 1.0
            lane_rows[2, neg] = -1.0
            lane_rows[3, pos] = 1.0
    lane_rows[4] = (np.arange(d) % LANES) < HEAD_DIM
    return jnp.asarray(place, BF16), jnp.asarray(lane_rows, F32)


def _inproj_kernel(h_ref, wf_ref, wq_ref, wk_ref, wvt_ref, bf_ref, c0_ref, place_ref, lanes_ref,
                   qe_ref, qo_ref, ke_ref, ko_ref, vt_ref, c_ref, carry_ref, *, tm):
    @pl.when(pl.program_id(1) == 0)
    def _():
        carry_ref[...] = c0_ref[...]

    hb = h_ref[...].astype(BF16)
    z = jnp.dot(hb, wf_ref[...], preferred_element_type=F32) + bf_ref[...]

    vt = lax.dot_general(wvt_ref[...], hb, (((1,), (1,)), ((), ())), preferred_element_type=F32)
    extra = jnp.where(lax.broadcasted_iota(jnp.int32, (VT_ROWS - HEAD_DIM, tm), 0) == 0, 1.0, 0.0)
    rows = []
    for h in range(N_HEADS):
        rows += [vt[h * HEAD_DIM:(h + 1) * HEAD_DIM], extra]
    vt_ref[...] = jnp.concatenate(rows, axis=0).astype(BF16)

    logf = (jnp.minimum(z, 0.0) - jnp.log1p(jnp.exp(-jnp.abs(z)))) * LOG2E
    tri = (lax.broadcasted_iota(jnp.int32, (tm, tm), 0)
           >= lax.broadcasted_iota(jnp.int32, (tm, tm), 1)).astype(BF16)
    cum = sum(jnp.dot(tri, piece, preferred_element_type=F32) for piece in _bf16_split(logf))
    c = carry_ref[...] + cum
    c_ref[...] = c
    carry_ref[...] = c[tm - 1:tm, :]

    q = jnp.dot(hb, wq_ref[...], preferred_element_type=F32) * (HEAD_DIM ** -0.5 * LOG2E)

    head_lane = lax.broadcasted_iota(jnp.int32, (tm, LANES), 1) < N_HEADS
    hi, mid, lo = (jnp.where(head_lane, piece.astype(F32), 0.0) for piece in _bf16_split(c))
    packed = hi + pltpu.roll(mid, N_HEADS, 1) + pltpu.roll(lo, 2 * N_HEADS, 1)
    placed = jnp.dot(packed.astype(BF16), place_ref[...], preferred_element_type=F32)
    bias_q = placed * lanes_ref[0:1, :] + lanes_ref[1:2, :]
    bias_k = placed * lanes_ref[2:3, :] + lanes_ref[3:4, :]
    first = lanes_ref[4:5, :] > 0.5

    qe_ref[...] = jnp.where(first, q, bias_q).astype(BF16)
    qo_ref[...] = jnp.where(first, bias_q, q).astype(BF16)
    k = jnp.dot(hb, wk_ref[...], preferred_element_type=F32)
    ke_ref[...] = jnp.where(first, k, bias_k).astype(BF16)
    ko_ref[...] = jnp.where(first, bias_k, k).astype(BF16)


def _inproj(h, wf, wq, wk, wvt, bfp, c0, place, lane_rows, *, tm):
    bsz, s, d = h.shape
    row = pl.BlockSpec((None, tm, d), lambda bi, j: (bi, j, 0))
    act = jax.ShapeDtypeStruct((bsz, s, d), BF16)
    wspec = _const_spec((d, d), True)
    return pl.pallas_call(
        functools.partial(_inproj_kernel, tm=tm),
        grid=(bsz, s // tm),
        in_specs=[row, _const_spec((d, LANES), True), wspec, wspec, wspec,
                  _const_spec((1, LANES)), _const_spec((1, LANES)), _const_spec((LANES, d)),
                  _const_spec((8, d))],
        out_specs=[row, row, row, row,
                   pl.BlockSpec((None, N_HEADS * VT_ROWS, tm), lambda bi, j: (bi, 0, j)),
                   pl.BlockSpec((None, tm, LANES), lambda bi, j: (bi, j, 0))],
        out_shape=[act, act, act, act,
                   jax.ShapeDtypeStruct((bsz, N_HEADS * VT_ROWS, s), BF16),
                   jax.ShapeDtypeStruct((bsz, s, LANES), F32)],
        scratch_shapes=[pltpu.VMEM((1, LANES), F32)],
        compiler_params=pltpu.CompilerParams(dimension_semantics=("arbitrary", "arbitrary"),
                                             vmem_limit_bytes=56 << 20),
        name=f"inproj_{bsz * s}",
    )(h, wf, wq, wk, wvt, bfp, c0, place, lane_rows)


def _attn_kernel(qe_ref, qo_ref, ke_ref, ko_ref, vt_ref, kme_ref, kmo_ref, vtm_ref, o_ref,
                 s_sc, mb_sc, p_sc, pm_sc, al_sc, m_sc, acc_sc, *, tq):
    nq = o_ref.shape[0] // tq
    nh = acc_sc.shape[0]

    def rows(i):
        return pl.ds(pl.multiple_of(i * tq, tq), tq)

    def slab(g):
        return slice((g // 2) * LANES, (g // 2 + 1) * LANES)

    def q_tile(g, jq):
        return (qe_ref, qo_ref)[g % 2][rows(jq), slab(g)]

    def k_chunk(g, kc):
        return (ke_ref, ko_ref)[g % 2][rows(kc), slab(g)]

    def k_meta(g):
        return (kme_ref, kmo_ref)[g % 2][:, slab(g)]

    def vt_rows(ref, g, cols):
        return ref[g * VT_ROWS:(g + 1) * VT_ROWS, cols]

    def dot_nt(a, b):
        return lax.dot_general(a, b, (((1,), (1,)), ((), ())), preferred_element_type=F32)

    causal_t = (lax.broadcasted_iota(jnp.int32, (tq, tq), 0)
                <= lax.broadcasted_iota(jnp.int32, (tq, tq), 1))
    is_meta = lax.broadcasted_iota(jnp.int32, (META_ROWS, tq), 0) < N_META

    def stage_a(jq, kc):
        for g in range(nh):
            st = dot_nt(k_chunk(g, kc), q_tile(g, jq))
            s_sc[g] = st
            mb_sc[g] = jnp.max(st, axis=0, keepdims=True)

    def stage_b(jq, first, diag):
        for g in range(nh):
            st = s_sc[g]
            if diag:
                st = jnp.where(causal_t, st, MASK_VALUE)
                sm = jnp.where(is_meta, dot_nt(k_meta(g), q_tile(g, jq)), MASK_VALUE)
                m_blk = jnp.maximum(jnp.max(st, axis=0, keepdims=True), jnp.max(sm, axis=0, keepdims=True))
            else:
                m_blk = mb_sc[g]
            if first:
                m_new = m_blk
                al_sc[g] = jnp.zeros((1, tq), F32)
            else:
                m_old = m_sc[g]
                m_new = jnp.maximum(m_old, m_blk)
                al_sc[g] = jnp.exp2(m_old - m_new)
            m_sc[g] = m_new
            p_sc[g] = jnp.exp2(st - m_new).astype(BF16)
            if diag:
                pm_sc[g] = jnp.exp2(sm - m_new).astype(BF16)

    def stage_c(jq, kc, diag):
        outs = []
        for g in range(nh):
            pv = jnp.dot(vt_rows(vt_ref, g, rows(kc)), p_sc[g], preferred_element_type=F32)
            if diag:
                pv = pv + jnp.dot(vt_rows(vtm_ref, g, slice(None)), pm_sc[g], preferred_element_type=F32)
            acc = al_sc[g] * acc_sc[g] + pv
            acc_sc[g] = acc
            if diag:
                outs.append(acc[:HEAD_DIM] / acc[HEAD_DIM:HEAD_DIM + 1])
        for g in range(0, len(outs), 2):
            o_ref[rows(jq), slab(g)] = jnp.concatenate(outs[g:g + 2], axis=0).T.astype(BF16)

    acc_sc[...] = jnp.zeros(acc_sc.shape, F32)
    stage_a(0, 0)
    stage_b(0, True, True)
    stage_a(1, 0)

    def tile(jq, carry):
        stage_c(jq - 1, jq - 1, True)
        stage_b(jq, True, False)
        stage_a(jq, 1)

        def chunk(kc):
            stage_c(jq, kc - 1, False)
            stage_b(jq, False, False)
            stage_a(jq, kc + 1)

        odd = (jq - 1) % 2

        @pl.when(odd == 1)
        def _():
            chunk(1)

        def chunk_pair(i, c):
            kc = 1 + odd + 2 * i
            chunk(kc)
            chunk(kc + 1)
            return c

        lax.fori_loop(0, (jq - 1) // 2, chunk_pair, 0)
        stage_c(jq, jq - 1, False)
        stage_b(jq, False, True)
        stage_a(jnp.minimum(jq + 1, nq - 1), 0)
        return carry

    lax.fori_loop(1, nq, tile, 0)
    stage_c(nq - 1, nq - 1, True)


def _attention(qe, qo, ke, ko, vt, kme, kmo, vtm, *, tq, npair):
    bsz, s, d = ke.shape
    nh = 2 * npair
    width = npair * LANES
    seq = pl.BlockSpec((None, s, width), lambda bi, hp: (bi, 0, hp))
    kmspec = pl.BlockSpec((META_ROWS, width), lambda bi, hp: (0, hp))
    return pl.pallas_call(
        functools.partial(_attn_kernel, tq=tq),
        grid=(bsz, d // width),
        in_specs=[seq, seq, seq, seq,
                  pl.BlockSpec((None, nh * VT_ROWS, s), lambda bi, hp: (bi, hp, 0)),
                  kmspec, kmspec,
                  pl.BlockSpec((nh * VT_ROWS, META_ROWS), lambda bi, hp: (hp, 0))],
        out_specs=seq,
        out_shape=jax.ShapeDtypeStruct((bsz, s, d), BF16),
        scratch_shapes=[pltpu.VMEM((nh, tq, tq), F32), pltpu.VMEM((nh, 1, tq), F32),
                        pltpu.VMEM((nh, tq, tq), BF16), pltpu.VMEM((nh, META_ROWS, tq), BF16),
                        pltpu.VMEM((nh, 1, tq), F32), pltpu.VMEM((nh, 1, tq), F32),
                        pltpu.VMEM((nh, VT_ROWS, tq), F32)],
        compiler_params=pltpu.CompilerParams(dimension_semantics=("arbitrary", "arbitrary"),
                                             vmem_limit_bytes=56 << 20),
        name="fox_attn",
    )(qe, qo, ke, ko, vt, kme, kmo, vtm)


def kernel(x, meta_tokens, pool_w, pool_scale, fox_w_in, fox_b_f, fox_w_o, ffn_w_gate, ffn_w_up,
           ffn_w_down, ln_g, ln_b):
    bsz, s, d = x.shape
    f_dim = ffn_w_gate.shape[-1]
    tm = 1024
    sub = 256
    tq = 512
    fc = -(-f_dim // (2 * MXU_TILE)) * MXU_TILE
    row = lambda a: a.reshape(1, -1).astype(F32)

    meta_pad = jnp.pad(meta_tokens, ((0, META_ROWS - N_META), (0, 0)))
    wg = [ffn_w_gate[i].astype(BF16) for i in range(DEPTH)]
    wu = [ffn_w_up[i].astype(BF16) for i in range(DEPTH)]
    wd = [ffn_w_down[i].astype(BF16) for i in range(DEPTH)]

    h2x, h2m = _layer0(x, meta_tokens, meta_pad, pool_w[0].astype(BF16), row(pool_scale[0]),
                       row(ln_g[0, 0]), row(ln_b[0, 0]), wg[0], wu[0], wd[0], row(ln_g[0, 1]), row(ln_b[0, 1]),
                       tm=tm, sub=sub, fc=fc)
    h2x = h2x.reshape(bsz * s, d)

    w_in = fox_w_in[0]
    wq, wk = (w_in[:, i * d:(i + 1) * d].astype(BF16) for i in range(2))
    wvt = w_in[:, 2 * d:3 * d].T.astype(BF16)
    wf = jnp.pad(w_in[:, 3 * d:], ((0, 0), (0, LANES - N_HEADS))).astype(BF16)
    bfp = jnp.pad(fox_b_f[0], (0, LANES - N_HEADS)).reshape(1, LANES)
    place, lane_rows = _bias_constants(d)
    inproj = functools.partial(_inproj, wf=wf, wq=wq, wk=wk, wvt=wvt, bfp=bfp, place=place, lane_rows=lane_rows)
    _, _, kme, kmo, vtm, cm = inproj(h2m[None], c0=jnp.zeros((1, LANES), F32), tm=META_ROWS)
    qe, qo, ke, ko, vt, _ = inproj(h2x.reshape(bsz, s, d), c0=cm[0, N_META - 1:N_META, :], tm=512)
    o = _attention(qe, qo, ke, ko, vt, kme[0], kmo[0], vtm[0], tq=tq, npair=2)
    out = _layer1_tail(o.reshape(bsz * s, d), h2x, fox_w_o[0].astype(BF16), row(ln_g[1, 0]), row(ln_b[1, 0]),
                       wg[1], wu[1], wd[1], row(ln_g[1, 1]), row(ln_b[1, 1]), tm=tm, sub=sub, fc=fc)
    return out.reshape(bsz, s, d)
```

```python
import functools
import math

import jax
import jax.numpy as jnp
import numpy as np
from jax import lax
from jax.experimental import pallas as pl
from jax.experimental.pallas import tpu as pltpu

N_META = 16
N_HEADS = 16
HEAD_DIM = 64
POOL_WINDOWS = (2, 4, 8, 16)
MAX_WINDOW = max(POOL_WINDOWS)
DEPTH = 2
DN_ALPHA = (2.0 * DEPTH) ** 0.25
LN_EPS = 1e-5
LANES = 128
MXU_TILE = 256
VMEM_BUDGET = 60 << 20
META_ROWS = 128
MASK_VALUE = -1e30
LOG2E = math.log2(math.e)

F32 = jnp.float32
BF16 = jnp.bfloat16


def _layer_norm(z, g, b):
    mu = jnp.mean(z, axis=-1, keepdims=True)
    zc = z - mu
    var = jnp.mean(zc * zc, axis=-1, keepdims=True)
    return zc * lax.rsqrt(var + LN_EPS) * g + b


def _const_spec(shape, single_buffer=False):
    zeros = (0,) * len(shape)
    if single_buffer:
        return pl.BlockSpec(shape, lambda *_: zeros, pipeline_mode=pl.Buffered(1))
    return pl.BlockSpec(shape, lambda *_: zeros)


def _swiglu(hb, wg_ref, wu_ref, wd_ref, fc):
    f_dim = wg_ref.shape[1]
    out = None
    for lo in range(0, f_dim, fc):
        sl = slice(lo, min(lo + fc, f_dim))
        hg = jnp.dot(hb, wg_ref[:, sl], preferred_element_type=F32)
        hu = jnp.dot(hb, wu_ref[:, sl], preferred_element_type=F32)
        a = (jax.nn.silu(hg) * hu).astype(BF16)
        part = jnp.dot(a, wd_ref[sl, :], preferred_element_type=F32)
        out = part if out is None else out + part
    return out


def _ffn_ln(h, wg_ref, wu_ref, wd_ref, g_ref, b_ref, fc):
    f = _swiglu(h.astype(BF16), wg_ref, wu_ref, wd_ref, fc)
    return _layer_norm(DN_ALPHA * h + f, g_ref[...], b_ref[...])


def _pool_mix(ext_ref, row0, rows, t1, w_ref, sc_ref):
    d = ext_ref.shape[-1]
    cgrp = d // len(POOL_WINDOWS)
    zs = []
    for g, w in enumerate(POOL_WINDOWS):
        sl = slice(g * cgrp, (g + 1) * cgrp)
        cur_g = ext_ref[pl.ds(MAX_WINDOW + row0, rows), sl]
        win = cur_g
        for i in range(1, w):
            win = win + ext_ref[pl.ds(MAX_WINDOW + row0 - i, rows), sl]
        y = win / jnp.minimum(t1, float(w)) - cur_g
        m = jnp.dot(y.astype(BF16), w_ref[g], preferred_element_type=F32) * sc_ref[:, sl]
        zs.append(DN_ALPHA * cur_g + m)
    return jnp.concatenate(zs, axis=1)


def _layer0_kernel(x_ref, halo_ref, meta_ref, pw_ref, psc_ref, g1_ref, b1_ref, wg_ref, wu_ref, wd_ref,
                   g2_ref, b2_ref, o_ref, ext_ref, *, tm, sub, is_meta, fc):
    d = x_ref.shape[-1]
    ext_ref[MAX_WINDOW:, :] = x_ref[...]
    if is_meta:
        ext_ref[:MAX_WINDOW, :] = jnp.zeros((MAX_WINDOW, d), F32)
        pos0 = 0
    else:
        j = pl.program_id(1)

        @pl.when(j == 0)
        def _():
            ext_ref[:MAX_WINDOW, :] = meta_ref[...]

        @pl.when(j > 0)
        def _():
            ext_ref[:MAX_WINDOW, :] = halo_ref[...]

        pos0 = N_META + j * tm
    def mixer(r):
        t1 = (pos0 + r + 1 + lax.broadcasted_iota(jnp.int32, (sub, 1), 0)).astype(F32)
        return _layer_norm(_pool_mix(ext_ref, r, sub, t1, pw_ref, psc_ref), g1_ref[...], b1_ref[...])

    h_next = mixer(0)
    for r in range(0, tm, sub):
        h1 = h_next
        if r + sub < tm:
            h_next = mixer(r + sub)
        o_ref[r:r + sub, :] = _ffn_ln(h1, wg_ref, wu_ref, wd_ref, g2_ref, b2_ref, fc)


def _layer1_tail_kernel(a_ref, h_ref, wo_ref, g1_ref, b1_ref, wg_ref, wu_ref, wd_ref, g2_ref, b2_ref,
                        o_ref, *, tm, sub, fc):
    def mixer(r):
        m = jnp.dot(a_ref[r:r + sub, :], wo_ref[...], preferred_element_type=F32)
        return _layer_norm(DN_ALPHA * h_ref[r:r + sub, :] + m, g1_ref[...], b1_ref[...])

    h_next = mixer(0)
    for r in range(0, tm, sub):
        h = h_next
        if r + sub < tm:
            h_next = mixer(r + sub)
        o_ref[r:r + sub, :] = _ffn_ln(h, wg_ref, wu_ref, wd_ref, g2_ref, b2_ref, fc)


def _ffn_vmem_limit(tm, sub, d, f, fc, extra):
    weights = 3 * d * f * 2
    tiles = 2 * 2 * tm * d * 4
    temps = 3 * (sub * fc * (4 + 4 + 4 + 2) + 4 * sub * d * 4)
    return min(weights + tiles + temps + extra + (6 << 20), VMEM_BUDGET)


def _layer_weight_spec(shape, layer):
    return pl.BlockSpec((None,) + shape, lambda *_: (layer, 0, 0), pipeline_mode=pl.Buffered(1))


def _layer0(x, meta_tokens, meta_pad, pw, psc, g1, b1, wg, wu, wd, g2, b2, *, layer, tm, sub, fc):
    bsz, s, d = x.shape
    f = wg.shape[-1]
    ngrp = len(POOL_WINDOWS)
    cgrp = d // ngrp
    hb = tm // MAX_WINDOW
    vec = _const_spec((1, d))
    consts = [_const_spec((ngrp, cgrp, cgrp)), vec, vec, vec,
              _layer_weight_spec((d, f), layer), _layer_weight_spec((d, f), layer),
              _layer_weight_spec((f, d), layer), vec, vec]
    args = (pw, psc, g1, b1, wg, wu, wd, g2, b2)
    hx = pl.pallas_call(
        functools.partial(_layer0_kernel, tm=tm, sub=sub, is_meta=False, fc=fc),
        grid=(bsz, s // tm),
        in_specs=[pl.BlockSpec((None, tm, d), lambda bi, j: (bi, j, 0)),
                  pl.BlockSpec((None, MAX_WINDOW, d), lambda bi, j: (bi, jnp.maximum(j * hb - 1, 0), 0)),
                  _const_spec((N_META, d))] + consts,
        out_specs=pl.BlockSpec((None, tm, d), lambda bi, j: (bi, j, 0)),
        out_shape=jax.ShapeDtypeStruct((bsz, s, d), F32),
        scratch_shapes=[pltpu.VMEM((MAX_WINDOW + tm, d), F32)],
        compiler_params=pltpu.CompilerParams(
            dimension_semantics=("arbitrary", "arbitrary"),
            vmem_limit_bytes=_ffn_vmem_limit(tm, sub, d, f, fc, (MAX_WINDOW + tm) * d * 4)),
        name="layer0_x",
    )(x, x, meta_tokens, *args)
    hm = pl.pallas_call(
        functools.partial(_layer0_kernel, tm=META_ROWS, sub=META_ROWS, is_meta=True, fc=fc),
        grid=(1,),
        in_specs=[_const_spec((META_ROWS, d)), _const_spec((MAX_WINDOW, d)), _const_spec((N_META, d))] + consts,
        out_specs=_const_spec((META_ROWS, d)),
        out_shape=jax.ShapeDtypeStruct((META_ROWS, d), F32),
        scratch_shapes=[pltpu.VMEM((MAX_WINDOW + META_ROWS, d), F32)],
        compiler_params=pltpu.CompilerParams(
            vmem_limit_bytes=_ffn_vmem_limit(META_ROWS, META_ROWS, d, f, fc, 0)),
        name="layer0_meta",
    )(meta_pad, meta_tokens, meta_tokens, *args)
    return hx, hm


def _layer1_tail(a, h, wo, g1, b1, wg, wu, wd, g2, b2, *, layer, tm, sub, fc):
    n, d = h.shape
    f = wg.shape[-1]
    row = pl.BlockSpec((tm, d), lambda i: (i, 0))
    vec = _const_spec((1, d))
    return pl.pallas_call(
        functools.partial(_layer1_tail_kernel, tm=tm, sub=sub, fc=fc),
        grid=(n // tm,),
        in_specs=[row, row, _const_spec((d, d), True), vec, vec,
                  _layer_weight_spec((d, f), layer), _layer_weight_spec((d, f), layer),
                  _layer_weight_spec((f, d), layer), vec, vec],
        out_specs=row,
        out_shape=jax.ShapeDtypeStruct((n, d), F32),
        compiler_params=pltpu.CompilerParams(
            dimension_semantics=("arbitrary",),
            vmem_limit_bytes=_ffn_vmem_limit(tm, sub, d, f, fc, d * d * 2 + 2 * tm * d * 2 + 2 * tm * d * 4)),
        name="layer1_tail",
    )(a, h, wo, g1, b1, wg, wu, wd, g2, b2)


BIAS_PIECES = 3
VT_ROWS = 80


def _bf16_split(x):
    hi = x.astype(BF16)
    r1 = x - hi.astype(F32)
    mid = r1.astype(BF16)
    lo = (r1 - mid.astype(F32)).astype(BF16)
    return hi, mid, lo


def _bias_constants(d):
    place = np.zeros((LANES, d), np.float32)
    lane_rows = np.zeros((8, d), np.float32)
    for h in range(N_HEADS):
        base = (h // 2) * LANES + (HEAD_DIM if h % 2 == 0 else 0)
        for piece in range(BIAS_PIECES):
            neg, pos = base + piece, base + BIAS_PIECES + piece
            place[piece * N_HEADS + h, [neg, pos]] = 1.0
            lane_rows[0, pos] = 1.0
            lane_rows[1, neg] = 1.0
            lane_rows[2, neg] = -1.0
            lane_rows[3, pos] = 1.0
    lane_rows[4] = (np.arange(d) % LANES) < HEAD_DIM
    return jnp.asarray(place, BF16), jnp.asarray(lane_rows, F32)


def _inproj_kernel(h_ref, wf_ref, wq_ref, wk_ref, wvt_ref, bf_ref, c0_ref, place_ref, lanes_ref,
                   qe_ref, qo_ref, ke_ref, ko_ref, vt_ref, c_ref, carry_ref, *, tm):
    @pl.when(pl.program_id(1) == 0)
    def _():
        carry_ref[...] = c0_ref[...]

    hb = h_ref[...].astype(BF16)
    z = jnp.dot(hb, wf_ref[...], preferred_element_type=F32) + bf_ref[...]

    vt = lax.dot_general(wvt_ref[...], hb, (((1,), (1,)), ((), ())), preferred_element_type=F32)
    extra = jnp.where(lax.broadcasted_iota(jnp.int32, (VT_ROWS - HEAD_DIM, tm), 0) == 0, 1.0, 0.0)
    rows = []
    for h in range(N_HEADS):
        rows += [vt[h * HEAD_DIM:(h + 1) * HEAD_DIM], extra]
    vt_ref[...] = jnp.concatenate(rows, axis=0).astype(BF16)

    logf = (jnp.minimum(z, 0.0) - jnp.log1p(jnp.exp(-jnp.abs(z)))) * LOG2E
    tri = (lax.broadcasted_iota(jnp.int32, (tm, tm), 0)
           >= lax.broadcasted_iota(jnp.int32, (tm, tm), 1)).astype(BF16)
    cum = sum(jnp.dot(tri, piece, preferred_element_type=F32) for piece in _bf16_split(logf))
    c = carry_ref[...] + cum
    c_ref[...] = c
    carry_ref[...] = c[tm - 1:tm, :]

    q = jnp.dot(hb, wq_ref[...], preferred_element_type=F32) * (HEAD_DIM ** -0.5 * LOG2E)

    head_lane = lax.broadcasted_iota(jnp.int32, (tm, LANES), 1) < N_HEADS
    hi, mid, lo = (jnp.where(head_lane, piece.astype(F32), 0.0) for piece in _bf16_split(c))
    packed = hi + pltpu.roll(mid, N_HEADS, 1) + pltpu.roll(lo, 2 * N_HEADS, 1)
    placed = jnp.dot(packed.astype(BF16), place_ref[...], preferred_element_type=F32)
    bias_q = placed * lanes_ref[0:1, :] + lanes_ref[1:2, :]
    bias_k = placed * lanes_ref[2:3, :] + lanes_ref[3:4, :]
    first = lanes_ref[4:5, :] > 0.5

    qe_ref[...] = jnp.where(first, q, bias_q).astype(BF16)
    qo_ref[...] = jnp.where(first, bias_q, q).astype(BF16)
    k = jnp.dot(hb, wk_ref[...], preferred_element_type=F32)
    ke_ref[...] = jnp.where(first, k, bias_k).astype(BF16)
    ko_ref[...] = jnp.where(first, bias_k, k).astype(BF16)


def _inproj(h, wf, wq, wk, wvt, bfp, c0, place, lane_rows, *, tm):
    bsz, s, d = h.shape
    row = pl.BlockSpec((None, tm, d), lambda bi, j: (bi, j, 0))
    act = jax.ShapeDtypeStruct((bsz, s, d), BF16)
    wspec = _const_spec((d, d), True)
    return pl.pallas_call(
        functools.partial(_inproj_kernel, tm=tm),
        grid=(bsz, s // tm),
        in_specs=[row, _const_spec((d, LANES), True), wspec, wspec, wspec,
                  _const_spec((1, LANES)), _const_spec((1, LANES)), _const_spec((LANES, d)),
                  _const_spec((8, d))],
        out_specs=[row, row, row, row,
                   pl.BlockSpec((None, N_HEADS * VT_ROWS, tm), lambda bi, j: (bi, 0, j)),
                   pl.BlockSpec((None, tm, LANES), lambda bi, j: (bi, j, 0))],
        out_shape=[act, act, act, act,
                   jax.ShapeDtypeStruct((bsz, N_HEADS * VT_ROWS, s), BF16),
                   jax.ShapeDtypeStruct((bsz, s, LANES), F32)],
        scratch_shapes=[pltpu.VMEM((1, LANES), F32)],
        compiler_params=pltpu.CompilerParams(dimension_semantics=("arbitrary", "arbitrary"),
                                             vmem_limit_bytes=56 << 20),
        name=f"inproj_{bsz * s}",
    )(h, wf, wq, wk, wvt, bfp, c0, place, lane_rows)


def _attn_kernel(qe_ref, qo_ref, ke_ref, ko_ref, vt_ref, kme_ref, kmo_ref, vtm_ref, o_ref,
                 s_sc, mb_sc, p_sc, pm_sc, al_sc, m_sc, acc_sc, *, tq):
    nq = o_ref.shape[0] // tq
    nh = acc_sc.shape[0]

    def rows(i):
        return pl.ds(pl.multiple_of(i * tq, tq), tq)

    def slab(g):
        return slice((g // 2) * LANES, (g // 2 + 1) * LANES)

    def q_tile(g, jq):
        return (qe_ref, qo_ref)[g % 2][rows(jq), slab(g)]

    def k_chunk(g, kc):
        return (ke_ref, ko_ref)[g % 2][rows(kc), slab(g)]

    def k_meta(g):
        return (kme_ref, kmo_ref)[g % 2][:, slab(g)]

    def vt_rows(ref, g, cols):
        return ref[g * VT_ROWS:(g + 1) * VT_ROWS, cols]

    def dot_nt(a, b):
        return lax.dot_general(a, b, (((1,), (1,)), ((), ())), preferred_element_type=F32)

    causal_t = (lax.broadcasted_iota(jnp.int32, (tq, tq), 0)
                <= lax.broadcasted_iota(jnp.int32, (tq, tq), 1))
    is_meta = lax.broadcasted_iota(jnp.int32, (META_ROWS, tq), 0) < N_META

    def put(ref, g, row):
        ref[g] = jnp.broadcast_to(row, (8, tq))

    def get(ref, g):
        return ref[g][0:1, :]

    def stage_a(jq, kc):
        for g in range(nh):
            st = dot_nt(k_chunk(g, kc), q_tile(g, jq))
            s_sc[g] = st
            put(mb_sc, g, jnp.max(st, axis=0, keepdims=True))

    def stage_b(jq, first, diag):
        for g in range(nh):
            st = s_sc[g]
            if diag:
                st = jnp.where(causal_t, st, MASK_VALUE)
                sm = jnp.where(is_meta, dot_nt(k_meta(g), q_tile(g, jq)), MASK_VALUE)
                m_blk = jnp.maximum(jnp.max(st, axis=0, keepdims=True), jnp.max(sm, axis=0, keepdims=True))
            else:
                m_blk = get(mb_sc, g)
            if first:
                m_new = m_blk
                put(al_sc, g, jnp.zeros((1, tq), F32))
            else:
                m_old = get(m_sc, g)
                m_new = jnp.maximum(m_old, m_blk)
                put(al_sc, g, jnp.exp2(m_old - m_new))
            put(m_sc, g, m_new)
            p_sc[g] = jnp.exp2(st - m_new).astype(BF16)
            if diag:
                pm_sc[g] = jnp.exp2(sm - m_new).astype(BF16)

    def stage_c(jq, kc, diag):
        outs = []
        for g in range(nh):
            pv = jnp.dot(vt_rows(vt_ref, g, rows(kc)), p_sc[g], preferred_element_type=F32)
            if diag:
                pv = pv + jnp.dot(vt_rows(vtm_ref, g, slice(None)), pm_sc[g], preferred_element_type=F32)
            acc = get(al_sc, g) * acc_sc[g] + pv
            acc_sc[g] = acc
            if diag:
                outs.append(acc[:HEAD_DIM] / acc[HEAD_DIM:HEAD_DIM + 1])
        for g in range(0, len(outs), 2):
            o_ref[rows(jq), slab(g)] = jnp.concatenate(outs[g:g + 2], axis=0).T.astype(BF16)

    def inner_chunks(jq):
        def chunk(kc):
            stage_c(jq, kc - 1, False)
            stage_b(jq, False, False)
            stage_a(jq, kc + 1)

        odd = (jq - 1) % 2

        @pl.when(odd == 1)
        def _():
            chunk(1)

        def chunk_pair(i, c):
            kc = 1 + odd + 2 * i
            chunk(kc)
            chunk(kc + 1)
            return c

        lax.fori_loop(0, (jq - 1) // 2, chunk_pair, 0)

    acc_sc[...] = jnp.zeros(acc_sc.shape, F32)
    stage_a(0, 0)
    stage_b(0, True, True)
    stage_a(1, 0)

    def tile(jq, carry):
        stage_c(jq - 1, jq - 1, True)
        stage_b(jq, True, False)
        stage_a(jq, 1)
        inner_chunks(jq)
        stage_c(jq, jq - 1, False)
        stage_b(jq, False, True)
        stage_a(jnp.minimum(jq + 1, nq - 1), 0)
        return carry

    lax.fori_loop(1, nq, tile, 0)
    stage_c(nq - 1, nq - 1, True)


def _attention(qe, qo, ke, ko, vt, kme, kmo, vtm, *, tq, npair):
    bsz, s, d = ke.shape
    nh = 2 * npair
    width = npair * LANES
    seq = pl.BlockSpec((None, s, width), lambda bi, hp: (bi, 0, hp))
    kmspec = pl.BlockSpec((META_ROWS, width), lambda bi, hp: (0, hp))
    return pl.pallas_call(
        functools.partial(_attn_kernel, tq=tq),
        grid=(bsz, d // width),
        in_specs=[seq, seq, seq, seq,
                  pl.BlockSpec((None, nh * VT_ROWS, s), lambda bi, hp: (bi, hp, 0)),
                  kmspec, kmspec,
                  pl.BlockSpec((nh * VT_ROWS, META_ROWS), lambda bi, hp: (hp, 0))],
        out_specs=seq,
        out_shape=jax.ShapeDtypeStruct((bsz, s, d), BF16),
        scratch_shapes=[pltpu.VMEM((nh, tq, tq), F32), pltpu.VMEM((nh, 8, tq), F32),
                        pltpu.VMEM((nh, tq, tq), BF16), pltpu.VMEM((nh, META_ROWS, tq), BF16),
                        pltpu.VMEM((nh, 8, tq), F32), pltpu.VMEM((nh, 8, tq), F32),
                        pltpu.VMEM((nh, VT_ROWS, tq), F32)],
        compiler_params=pltpu.CompilerParams(dimension_semantics=("arbitrary", "arbitrary"),
                                             vmem_limit_bytes=56 << 20),
        name="fox_attn",
    )(qe, qo, ke, ko, vt, kme, kmo, vtm)


def kernel(x, meta_tokens, pool_w, pool_scale, fox_w_in, fox_b_f, fox_w_o, ffn_w_gate, ffn_w_up,
           ffn_w_down, ln_g, ln_b):
    bsz, s, d = x.shape
    f_dim = ffn_w_gate.shape[-1]
    tm = 1024
    sub = 256
    tq = 512
    fc = -(-f_dim // (2 * MXU_TILE)) * MXU_TILE
    row = lambda a: a.reshape(1, -1).astype(F32)

    meta_pad = jnp.pad(meta_tokens, ((0, META_ROWS - N_META), (0, 0)))
    wg = ffn_w_gate.astype(BF16)
    wu = ffn_w_up.astype(BF16)
    wd = ffn_w_down.astype(BF16)

    h2x, h2m = _layer0(x, meta_tokens, meta_pad, pool_w[0].astype(BF16), row(pool_scale[0]),
                       row(ln_g[0, 0]), row(ln_b[0, 0]), wg, wu, wd, row(ln_g[0, 1]), row(ln_b[0, 1]),
                       layer=0, tm=tm, sub=sub, fc=fc)
    h2x = h2x.reshape(bsz * s, d)

    w_in = fox_w_in[0]
    wq, wk = (w_in[:, i * d:(i + 1) * d].astype(BF16) for i in range(2))
    wvt = w_in[:, 2 * d:3 * d].T.astype(BF16)
    wf = jnp.pad(w_in[:, 3 * d:], ((0, 0), (0, LANES - N_HEADS))).astype(BF16)
    bfp = jnp.pad(fox_b_f[0], (0, LANES - N_HEADS)).reshape(1, LANES)
    place, lane_rows = _bias_constants(d)
    inproj = functools.partial(_inproj, wf=wf, wq=wq, wk=wk, wvt=wvt, bfp=bfp, place=place, lane_rows=lane_rows)
    _, _, kme, kmo, vtm, cm = inproj(h2m[None], c0=jnp.zeros((1, LANES), F32), tm=META_ROWS)
    qe, qo, ke, ko, vt, _ = inproj(h2x.reshape(bsz, s, d), c0=cm[0, N_META - 1:N_META, :], tm=512)
    o = _attention(qe, qo, ke, ko, vt, kme[0], kmo[0], vtm[0], tq=tq, npair=2)
    out = _layer1_tail(o.reshape(bsz * s, d), h2x, fox_w_o[0].astype(BF16), row(ln_g[1, 0]), row(ln_b[1, 0]),
                       wg, wu, wd, row(ln_g[1, 1]), row(ln_b[1, 1]), layer=1, tm=tm, sub=sub, fc=fc)
    return out.reshape(bsz, s, d)
```

```python
import functools
import math

import jax
import jax.numpy as jnp
import numpy as np
from jax import lax
from jax.experimental import pallas as pl
from jax.experimental.pallas import tpu as pltpu

N_META = 16
N_HEADS = 16
HEAD_DIM = 64
POOL_WINDOWS = (2, 4, 8, 16)
MAX_WINDOW = max(POOL_WINDOWS)
DEPTH = 2
DN_ALPHA = (2.0 * DEPTH) ** 0.25
LN_EPS = 1e-5
LANES = 128
MXU_TILE = 256
VMEM_BUDGET = 60 << 20
META_ROWS = 128
MASK_VALUE = -1e30
LOG2E = math.log2(math.e)

F32 = jnp.float32
BF16 = jnp.bfloat16


def _layer_norm(z, g, b):
    mu = jnp.mean(z, axis=-1, keepdims=True)
    zc = z - mu
    var = jnp.mean(zc * zc, axis=-1, keepdims=True)
    return zc * lax.rsqrt(var + LN_EPS) * g + b


def _const_spec(shape, single_buffer=False):
    zeros = (0,) * len(shape)
    if single_buffer:
        return pl.BlockSpec(shape, lambda *_: zeros, pipeline_mode=pl.Buffered(1))
    return pl.BlockSpec(shape, lambda *_: zeros)


def _swiglu(hb, wg_ref, wu_ref, wd_ref, fc):
    f_dim = wg_ref.shape[1]
    out = None
    for lo in range(0, f_dim, fc):
        sl = slice(lo, min(lo + fc, f_dim))
        hg = jnp.dot(hb, wg_ref[:, sl], preferred_element_type=F32)
        hu = jnp.dot(hb, wu_ref[:, sl], preferred_element_type=F32)
        a = (jax.nn.silu(hg) * hu).astype(BF16)
        part = jnp.dot(a, wd_ref[sl, :], preferred_element_type=F32)
        out = part if out is None else out + part
    return out


def _ffn_ln(h, wg_ref, wu_ref, wd_ref, g_ref, b_ref, fc):
    f = _swiglu(h.astype(BF16), wg_ref, wu_ref, wd_ref, fc)
    return _layer_norm(DN_ALPHA * h + f, g_ref[...], b_ref[...])


POOL_HALO = 2 * MAX_WINDOW
SUBLANES = 8


def _pool_mix(ext_ref, lvl_ref, row0, rows, t1, w_ref, sc_ref):
    d = ext_ref.shape[-1]
    cgrp = d // len(POOL_WINDOWS)
    zs = []
    for g, w in enumerate(POOL_WINDOWS):
        sl = slice(g * cgrp, (g + 1) * cgrp)
        arr = ext_ref[pl.ds(row0, rows + POOL_HALO), sl]
        cur_g = arr[POOL_HALO:]
        used = 0
        for k in (1, 2, 4):
            if k >= w:
                break
            n = arr.shape[0]
            lvl_ref[g, :n, :] = arr
            arr = arr[SUBLANES:] + lvl_ref[g, pl.ds(SUBLANES - k, n - SUBLANES), :]
            used += SUBLANES
        if w > SUBLANES:
            arr = arr[SUBLANES:] + arr[:-SUBLANES]
            used += SUBLANES
        win = arr[POOL_HALO - used:]
        y = win / jnp.minimum(t1, float(w)) - cur_g
        m = jnp.dot(y.astype(BF16), w_ref[g], preferred_element_type=F32) * sc_ref[:, sl]
        zs.append(DN_ALPHA * cur_g + m)
    return jnp.concatenate(zs, axis=1)


def _layer0_kernel(x_ref, halo_ref, meta_ref, pw_ref, psc_ref, g1_ref, b1_ref, wg_ref, wu_ref, wd_ref,
                   g2_ref, b2_ref, o_ref, ext_ref, lvl_ref, *, tm, sub, is_meta, fc):
    d = x_ref.shape[-1]
    ext_ref[POOL_HALO:, :] = x_ref[...]
    if is_meta:
        ext_ref[:POOL_HALO, :] = jnp.zeros((POOL_HALO, d), F32)
        pos0 = 0
    else:
        j = pl.program_id(1)

        @pl.when(j == 0)
        def _():
            ext_ref[:POOL_HALO - N_META, :] = jnp.zeros((POOL_HALO - N_META, d), F32)
            ext_ref[POOL_HALO - N_META:POOL_HALO, :] = meta_ref[...]

        @pl.when(j > 0)
        def _():
            ext_ref[:POOL_HALO, :] = halo_ref[...]

        pos0 = N_META + j * tm

    def mixer(r):
        t1 = (pos0 + r + 1 + lax.broadcasted_iota(jnp.int32, (sub, 1), 0)).astype(F32)
        return _layer_norm(_pool_mix(ext_ref, lvl_ref, r, sub, t1, pw_ref, psc_ref), g1_ref[...], b1_ref[...])

    h_next = mixer(0)
    for r in range(0, tm, sub):
        h1 = h_next
        if r + sub < tm:
            h_next = mixer(r + sub)
        o_ref[r:r + sub, :] = _ffn_ln(h1, wg_ref, wu_ref, wd_ref, g2_ref, b2_ref, fc)


def _layer1_tail_kernel(a_ref, h_ref, wo_ref, g1_ref, b1_ref, wg_ref, wu_ref, wd_ref, g2_ref, b2_ref,
                        o_ref, *, tm, sub, fc):
    def mixer(r):
        m = jnp.dot(a_ref[r:r + sub, :], wo_ref[...], preferred_element_type=F32)
        return _layer_norm(DN_ALPHA * h_ref[r:r + sub, :] + m, g1_ref[...], b1_ref[...])

    h_next = mixer(0)
    for r in range(0, tm, sub):
        h = h_next
        if r + sub < tm:
            h_next = mixer(r + sub)
        o_ref[r:r + sub, :] = _ffn_ln(h, wg_ref, wu_ref, wd_ref, g2_ref, b2_ref, fc)


def _ffn_vmem_limit(tm, sub, d, f, fc, extra):
    weights = 3 * d * f * 2
    tiles = 2 * 2 * tm * d * 4
    temps = 3 * (sub * fc * (4 + 4 + 4 + 2) + 4 * sub * d * 4)
    return min(weights + tiles + temps + extra + (6 << 20), VMEM_BUDGET)


def _layer_weight_spec(shape, layer):
    return pl.BlockSpec((None,) + shape, lambda *_: (layer, 0, 0), pipeline_mode=pl.Buffered(1))


def _layer0(x, meta_tokens, meta_pad, pw, psc, g1, b1, wg, wu, wd, g2, b2, *, layer, tm, sub, fc):
    bsz, s, d = x.shape
    f = wg.shape[-1]
    ngrp = len(POOL_WINDOWS)
    cgrp = d // ngrp
    hb = tm // POOL_HALO
    assert all(w & (w - 1) == 0 and w <= MAX_WINDOW == N_META for w in POOL_WINDOWS)
    vec = _const_spec((1, d))
    consts = [_const_spec((ngrp, cgrp, cgrp)), vec, vec, vec,
              _layer_weight_spec((d, f), layer), _layer_weight_spec((d, f), layer),
              _layer_weight_spec((f, d), layer), vec, vec]
    args = (pw, psc, g1, b1, wg, wu, wd, g2, b2)
    hx = pl.pallas_call(
        functools.partial(_layer0_kernel, tm=tm, sub=sub, is_meta=False, fc=fc),
        grid=(bsz, s // tm),
        in_specs=[pl.BlockSpec((None, tm, d), lambda bi, j: (bi, j, 0)),
                  pl.BlockSpec((None, POOL_HALO, d), lambda bi, j: (bi, jnp.maximum(j * hb - 1, 0), 0)),
                  _const_spec((N_META, d))] + consts,
        out_specs=pl.BlockSpec((None, tm, d), lambda bi, j: (bi, j, 0)),
        out_shape=jax.ShapeDtypeStruct((bsz, s, d), F32),
        scratch_shapes=[pltpu.VMEM((POOL_HALO + tm, d), F32), pltpu.VMEM((ngrp, POOL_HALO + sub, cgrp), F32)],
        compiler_params=pltpu.CompilerParams(
            dimension_semantics=("arbitrary", "arbitrary"),
            vmem_limit_bytes=_ffn_vmem_limit(tm, sub, d, f, fc, (2 * POOL_HALO + tm + sub) * d * 4)),
        name="layer0_x",
    )(x, x, meta_tokens, *args)
    hm = pl.pallas_call(
        functools.partial(_layer0_kernel, tm=META_ROWS, sub=META_ROWS, is_meta=True, fc=fc),
        grid=(1,),
        in_specs=[_const_spec((META_ROWS, d)), _const_spec((N_META, d)), _const_spec((N_META, d))] + consts,
        out_specs=_const_spec((META_ROWS, d)),
        out_shape=jax.ShapeDtypeStruct((META_ROWS, d), F32),
        scratch_shapes=[pltpu.VMEM((POOL_HALO + META_ROWS, d), F32),
                        pltpu.VMEM((ngrp, POOL_HALO + META_ROWS, cgrp), F32)],
        compiler_params=pltpu.CompilerParams(
            vmem_limit_bytes=_ffn_vmem_limit(META_ROWS, META_ROWS, d, f, fc, 0)),
        name="layer0_meta",
    )(meta_pad, meta_tokens, meta_tokens, *args)
    return hx, hm


def _layer1_tail(a, h, wo, g1, b1, wg, wu, wd, g2, b2, *, layer, tm, sub, fc):
    n, d = h.shape
    f = wg.shape[-1]
    row = pl.BlockSpec((tm, d), lambda i: (i, 0))
    vec = _const_spec((1, d))
    return pl.pallas_call(
        functools.partial(_layer1_tail_kernel, tm=tm, sub=sub, fc=fc),
        grid=(n // tm,),
        in_specs=[row, row, _const_spec((d, d), True), vec, vec,
                  _layer_weight_spec((d, f), layer), _layer_weight_spec((d, f), layer),
                  _layer_weight_spec((f, d), layer), vec, vec],
        out_specs=row,
        out_shape=jax.ShapeDtypeStruct((n, d), F32),
        compiler_params=pltpu.CompilerParams(
            dimension_semantics=("arbitrary",),
            vmem_limit_bytes=_ffn_vmem_limit(tm, sub, d, f, fc, d * d * 2 + 2 * tm * d * 2 + 2 * tm * d * 4)),
        name="layer1_tail",
    )(a, h, wo, g1, b1, wg, wu, wd, g2, b2)


BIAS_PIECES = 3
VT_ROWS = 80


def _bf16_split(x):
    hi = x.astype(BF16)
    r1 = x - hi.astype(F32)
    mid = r1.astype(BF16)
    lo = (r1 - mid.astype(F32)).astype(BF16)
    return hi, mid, lo


def _bias_constants(d):
    place = np.zeros((LANES, d), np.float32)
    lane_rows = np.zeros((8, d), np.float32)
    for h in range(N_HEADS):
        base = (h // 2) * LANES + (HEAD_DIM if h % 2 == 0 else 0)
        for piece in range(BIAS_PIECES):
            neg, pos = base + piece, base + BIAS_PIECES + piece
            place[piece * N_HEADS + h, [neg, pos]] = 1.0
            lane_rows[0, pos] = 1.0
            lane_rows[1, neg] = 1.0
            lane_rows[2, neg] = -1.0
            lane_rows[3, pos] = 1.0
    lane_rows[4] = (np.arange(d) % LANES) < HEAD_DIM
    return jnp.asarray(place, BF16), jnp.asarray(lane_rows, F32)


def _inproj_kernel(h_ref, wf_ref, wq_ref, wk_ref, wvt_ref, bf_ref, c0_ref, place_ref, lanes_ref,
                   qe_ref, qo_ref, ke_ref, ko_ref, vt_ref, c_ref, carry_ref, *, tm):
    @pl.when(pl.program_id(1) == 0)
    def _():
        carry_ref[...] = c0_ref[...]

    hb = h_ref[...].astype(BF16)
    z = jnp.dot(hb, wf_ref[...], preferred_element_type=F32) + bf_ref[...]

    vt = lax.dot_general(wvt_ref[...], hb, (((1,), (1,)), ((), ())), preferred_element_type=F32)
    extra = jnp.where(lax.broadcasted_iota(jnp.int32, (VT_ROWS - HEAD_DIM, tm), 0) == 0, 1.0, 0.0)
    rows = []
    for h in range(N_HEADS):
        rows += [vt[h * HEAD_DIM:(h + 1) * HEAD_DIM], extra]
    vt_ref[...] = jnp.concatenate(rows, axis=0).astype(BF16)

    logf = (jnp.minimum(z, 0.0) - jnp.log1p(jnp.exp(-jnp.abs(z)))) * LOG2E
    tri = (lax.broadcasted_iota(jnp.int32, (tm, tm), 0)
           >= lax.broadcasted_iota(jnp.int32, (tm, tm), 1)).astype(BF16)
    cum = sum(jnp.dot(tri, piece, preferred_element_type=F32) for piece in _bf16_split(logf))
    c = carry_ref[...] + cum
    c_ref[...] = c
    carry_ref[...] = c[tm - 1:tm, :]

    q = jnp.dot(hb, wq_ref[...], preferred_element_type=F32) * (HEAD_DIM ** -0.5 * LOG2E)

    head_lane = lax.broadcasted_iota(jnp.int32, (tm, LANES), 1) < N_HEADS
    hi, mid, lo = (jnp.where(head_lane, piece.astype(F32), 0.0) for piece in _bf16_split(c))
    packed = hi + pltpu.roll(mid, N_HEADS, 1) + pltpu.roll(lo, 2 * N_HEADS, 1)
    placed = jnp.dot(packed.astype(BF16), place_ref[...], preferred_element_type=F32)
    bias_q = placed * lanes_ref[0:1, :] + lanes_ref[1:2, :]
    bias_k = placed * lanes_ref[2:3, :] + lanes_ref[3:4, :]
    first = lanes_ref[4:5, :] > 0.5

    qe_ref[...] = jnp.where(first, q, bias_q).astype(BF16)
    qo_ref[...] = jnp.where(first, bias_q, q).astype(BF16)
    k = jnp.dot(hb, wk_ref[...], preferred_element_type=F32)
    ke_ref[...] = jnp.where(first, k, bias_k).astype(BF16)
    ko_ref[...] = jnp.where(first, bias_k, k).astype(BF16)


def _inproj(h, wf, wq, wk, wvt, bfp, c0, place, lane_rows, *, tm):
    bsz, s, d = h.shape
    row = pl.BlockSpec((None, tm, d), lambda bi, j: (bi, j, 0))
    act = jax.ShapeDtypeStruct((bsz, s, d), BF16)
    wspec = _const_spec((d, d), True)
    return pl.pallas_call(
        functools.partial(_inproj_kernel, tm=tm),
        grid=(bsz, s // tm),
        in_specs=[row, _const_spec((d, LANES), True), wspec, wspec, wspec,
                  _const_spec((1, LANES)), _const_spec((1, LANES)), _const_spec((LANES, d)),
                  _const_spec((8, d))],
        out_specs=[row, row, row, row,
                   pl.BlockSpec((None, N_HEADS * VT_ROWS, tm), lambda bi, j: (bi, 0, j)),
                   pl.BlockSpec((None, tm, LANES), lambda bi, j: (bi, j, 0))],
        out_shape=[act, act, act, act,
                   jax.ShapeDtypeStruct((bsz, N_HEADS * VT_ROWS, s), BF16),
                   jax.ShapeDtypeStruct((bsz, s, LANES), F32)],
        scratch_shapes=[pltpu.VMEM((1, LANES), F32)],
        compiler_params=pltpu.CompilerParams(dimension_semantics=("arbitrary", "arbitrary"),
                                             vmem_limit_bytes=56 << 20),
        name=f"inproj_{bsz * s}",
    )(h, wf, wq, wk, wvt, bfp, c0, place, lane_rows)


def _attn_kernel(qe_ref, qo_ref, ke_ref, ko_ref, vt_ref, kme_ref, kmo_ref, vtm_ref, o_ref,
                 s_sc, mb_sc, p_sc, pm_sc, al_sc, m_sc, acc_sc, *, tq):
    nq = o_ref.shape[0] // tq
    nh = acc_sc.shape[0]

    def rows(i):
        return pl.ds(pl.multiple_of(i * tq, tq), tq)

    def slab(g):
        return slice((g // 2) * LANES, (g // 2 + 1) * LANES)

    def q_tile(g, jq):
        return (qe_ref, qo_ref)[g % 2][rows(jq), slab(g)]

    def k_chunk(g, kc):
        return (ke_ref, ko_ref)[g % 2][rows(kc), slab(g)]

    def k_meta(g):
        return (kme_ref, kmo_ref)[g % 2][:, slab(g)]

    def vt_rows(ref, g, cols):
        return ref[g * VT_ROWS:(g + 1) * VT_ROWS, cols]

    def dot_nt(a, b):
        return lax.dot_general(a, b, (((1,), (1,)), ((), ())), preferred_element_type=F32)

    causal_t = (lax.broadcasted_iota(jnp.int32, (tq, tq), 0)
                <= lax.broadcasted_iota(jnp.int32, (tq, tq), 1))
    is_meta = lax.broadcasted_iota(jnp.int32, (META_ROWS, tq), 0) < N_META

    def put(ref, g, row):
        ref[g] = jnp.broadcast_to(row, (8, tq))

    def get(ref, g):
        return ref[g][0:1, :]

    def stage_a(jq, kc):
        for g in range(nh):
            st = dot_nt(k_chunk(g, kc), q_tile(g, jq))
            s_sc[g] = st
            put(mb_sc, g, jnp.max(st, axis=0, keepdims=True))

    def stage_b(jq, first, diag):
        for g in range(nh):
            st = s_sc[g]
            if diag:
                st = jnp.where(causal_t, st, MASK_VALUE)
                sm = jnp.where(is_meta, dot_nt(k_meta(g), q_tile(g, jq)), MASK_VALUE)
                m_blk = jnp.maximum(jnp.max(st, axis=0, keepdims=True), jnp.max(sm, axis=0, keepdims=True))
            else:
                m_blk = get(mb_sc, g)
            if first:
                m_new = m_blk
                put(al_sc, g, jnp.zeros((1, tq), F32))
            else:
                m_old = get(m_sc, g)
                m_new = jnp.maximum(m_old, m_blk)
                put(al_sc, g, jnp.exp2(m_old - m_new))
            put(m_sc, g, m_new)
            p_sc[g] = jnp.exp2(st - m_new).astype(BF16)
            if diag:
                pm_sc[g] = jnp.exp2(sm - m_new).astype(BF16)

    def stage_c(jq, kc, diag):
        outs = []
        for g in range(nh):
            pv = jnp.dot(vt_rows(vt_ref, g, rows(kc)), p_sc[g], preferred_element_type=F32)
            if diag:
                pv = pv + jnp.dot(vt_rows(vtm_ref, g, slice(None)), pm_sc[g], preferred_element_type=F32)
            acc = get(al_sc, g) * acc_sc[g] + pv
            acc_sc[g] = acc
            if diag:
                outs.append(acc[:HEAD_DIM] / acc[HEAD_DIM:HEAD_DIM + 1])
        for g in range(0, len(outs), 2):
            o_ref[rows(jq), slab(g)] = jnp.concatenate(outs[g:g + 2], axis=0).T.astype(BF16)

    def inner_chunks(jq):
        def chunk(kc):
            stage_c(jq, kc - 1, False)
            stage_b(jq, False, False)
            stage_a(jq, kc + 1)

        odd = (jq - 1) % 2

        @pl.when(odd == 1)
        def _():
            chunk(1)

        def chunk_pair(i, c):
            kc = 1 + odd + 2 * i
            chunk(kc)
            chunk(kc + 1)
            return c

        lax.fori_loop(0, (jq - 1) // 2, chunk_pair, 0)

    acc_sc[...] = jnp.zeros(acc_sc.shape, F32)
    stage_a(0, 0)
    stage_b(0, True, True)
    stage_a(1, 0)

    def tile(jq, carry):
        stage_c(jq - 1, jq - 1, True)
        stage_b(jq, True, False)
        stage_a(jq, 1)
        inner_chunks(jq)
        stage_c(jq, jq - 1, False)
        stage_b(jq, False, True)
        stage_a(jnp.minimum(jq + 1, nq - 1), 0)
        return carry

    lax.fori_loop(1, nq, tile, 0)
    stage_c(nq - 1, nq - 1, True)


def _attention(qe, qo, ke, ko, vt, kme, kmo, vtm, *, tq, npair):
    bsz, s, d = ke.shape
    nh = 2 * npair
    width = npair * LANES
    seq = pl.BlockSpec((None, s, width), lambda bi, hp: (bi, 0, hp))
    kmspec = pl.BlockSpec((META_ROWS, width), lambda bi, hp: (0, hp))
    return pl.pallas_call(
        functools.partial(_attn_kernel, tq=tq),
        grid=(bsz, d // width),
        in_specs=[seq, seq, seq, seq,
                  pl.BlockSpec((None, nh * VT_ROWS, s), lambda bi, hp: (bi, hp, 0)),
                  kmspec, kmspec,
                  pl.BlockSpec((nh * VT_ROWS, META_ROWS), lambda bi, hp: (hp, 0))],
        out_specs=seq,
        out_shape=jax.ShapeDtypeStruct((bsz, s, d), BF16),
        scratch_shapes=[pltpu.VMEM((nh, tq, tq), F32), pltpu.VMEM((nh, 8, tq), F32),
                        pltpu.VMEM((nh, tq, tq), BF16), pltpu.VMEM((nh, META_ROWS, tq), BF16),
                        pltpu.VMEM((nh, 8, tq), F32), pltpu.VMEM((nh, 8, tq), F32),
                        pltpu.VMEM((nh, VT_ROWS, tq), F32)],
        compiler_params=pltpu.CompilerParams(dimension_semantics=("arbitrary", "arbitrary"),
                                             vmem_limit_bytes=56 << 20),
        name="fox_attn",
    )(qe, qo, ke, ko, vt, kme, kmo, vtm)


def kernel(x, meta_tokens, pool_w, pool_scale, fox_w_in, fox_b_f, fox_w_o, ffn_w_gate, ffn_w_up,
           ffn_w_down, ln_g, ln_b):
    bsz, s, d = x.shape
    f_dim = ffn_w_gate.shape[-1]
    tm = 1024
    sub = 256
    tq = 512
    fc = -(-f_dim // (2 * MXU_TILE)) * MXU_TILE
    row = lambda a: a.reshape(1, -1).astype(F32)

    meta_pad = jnp.pad(meta_tokens, ((0, META_ROWS - N_META), (0, 0)))
    wg = ffn_w_gate.astype(BF16)
    wu = ffn_w_up.astype(BF16)
    wd = ffn_w_down.astype(BF16)

    h2x, h2m = _layer0(x, meta_tokens, meta_pad, pool_w[0].astype(BF16), row(pool_scale[0]),
                       row(ln_g[0, 0]), row(ln_b[0, 0]), wg, wu, wd, row(ln_g[0, 1]), row(ln_b[0, 1]),
                       layer=0, tm=tm, sub=sub, fc=fc)
    h2x = h2x.reshape(bsz * s, d)

    w_in = fox_w_in[0]
    wq, wk = (w_in[:, i * d:(i + 1) * d].astype(BF16) for i in range(2))
    wvt = w_in[:, 2 * d:3 * d].T.astype(BF16)
    wf = jnp.pad(w_in[:, 3 * d:], ((0, 0), (0, LANES - N_HEADS))).astype(BF16)
    bfp = jnp.pad(fox_b_f[0], (0, LANES - N_HEADS)).reshape(1, LANES)
    place, lane_rows = _bias_constants(d)
    inproj = functools.partial(_inproj, wf=wf, wq=wq, wk=wk, wvt=wvt, bfp=bfp, place=place, lane_rows=lane_rows)
    _, _, kme, kmo, vtm, cm = inproj(h2m[None], c0=jnp.zeros((1, LANES), F32), tm=META_ROWS)
    qe, qo, ke, ko, vt, _ = inproj(h2x.reshape(bsz, s, d), c0=cm[0, N_META - 1:N_META, :], tm=512)
    o = _attention(qe, qo, ke, ko, vt, kme[0], kmo[0], vtm[0], tq=tq, npair=2)
    out = _layer1_tail(o.reshape(bsz * s, d), h2x, fox_w_o[0].astype(BF16), row(ln_g[1, 0]), row(ln_b[1, 0]),
                       wg, wu, wd, row(ln_g[1, 1]), row(ln_b[1, 1]), layer=1, tm=tm, sub=sub, fc=fc)
    return out.reshape(bsz, s, d)
```

```python
import functools
import math

import jax
import jax.numpy as jnp
import numpy as np
from jax import lax
from jax.experimental import pallas as pl
from jax.experimental.pallas import tpu as pltpu

N_META = 16
N_HEADS = 16
HEAD_DIM = 64
POOL_WINDOWS = (2, 4, 8, 16)
MAX_WINDOW = max(POOL_WINDOWS)
DEPTH = 2
DN_ALPHA = (2.0 * DEPTH) ** 0.25
LN_EPS = 1e-5
LANES = 128
MXU_TILE = 256
VMEM_BUDGET = 60 << 20
META_ROWS = 128
MASK_VALUE = -1e30
LOG2E = math.log2(math.e)

F32 = jnp.float32
BF16 = jnp.bfloat16


def _layer_norm(z, g, b):
    mu = jnp.mean(z, axis=-1, keepdims=True)
    zc = z - mu
    var = jnp.mean(zc * zc, axis=-1, keepdims=True)
    return zc * lax.rsqrt(var + LN_EPS) * g + b


def _const_spec(shape, single_buffer=False):
    zeros = (0,) * len(shape)
    if single_buffer:
        return pl.BlockSpec(shape, lambda *_: zeros, pipeline_mode=pl.Buffered(1))
    return pl.BlockSpec(shape, lambda *_: zeros)


def _swiglu(hb, wg_ref, wu_ref, wd_ref, fc):
    f_dim = wg_ref.shape[1]
    out = None
    for lo in range(0, f_dim, fc):
        sl = slice(lo, min(lo + fc, f_dim))
        hg = jnp.dot(hb, wg_ref[:, sl], preferred_element_type=F32)
        hu = jnp.dot(hb, wu_ref[:, sl], preferred_element_type=F32)
        a = (jax.nn.silu(hg) * hu).astype(BF16)
        part = jnp.dot(a, wd_ref[sl, :], preferred_element_type=F32)
        out = part if out is None else out + part
    return out


def _ffn_ln(h, wg_ref, wu_ref, wd_ref, g_ref, b_ref, fc):
    f = _swiglu(h.astype(BF16), wg_ref, wu_ref, wd_ref, fc)
    return _layer_norm(DN_ALPHA * h + f, g_ref[...], b_ref[...])


POOL_HALO = 2 * MAX_WINDOW
SUBLANES = 8


def _pool_mix(ext_ref, lvl_ref, row0, rows, t1, w_ref, sc_ref):
    d = ext_ref.shape[-1]
    cgrp = d // len(POOL_WINDOWS)
    zs = []
    for g, w in enumerate(POOL_WINDOWS):
        sl = slice(g * cgrp, (g + 1) * cgrp)
        arr = ext_ref[pl.ds(row0, rows + POOL_HALO), sl]
        cur_g = arr[POOL_HALO:]
        used = 0
        for k in (1, 2, 4):
            if k >= w:
                break
            n = arr.shape[0]
            lvl_ref[g, :n, :] = arr
            arr = arr[SUBLANES:] + lvl_ref[g, pl.ds(SUBLANES - k, n - SUBLANES), :]
            used += SUBLANES
        if w > SUBLANES:
            arr = arr[SUBLANES:] + arr[:-SUBLANES]
            used += SUBLANES
        win = arr[POOL_HALO - used:]
        y = win / jnp.minimum(t1, float(w)) - cur_g
        m = jnp.dot(y.astype(BF16), w_ref[g], preferred_element_type=F32) * sc_ref[:, sl]
        zs.append(DN_ALPHA * cur_g + m)
    return jnp.concatenate(zs, axis=1)


def _layer0_kernel(x_ref, halo_ref, meta_ref, pw_ref, psc_ref, g1_ref, b1_ref, wg_ref, wu_ref, wd_ref,
                   g2_ref, b2_ref, o_ref, ext_ref, lvl_ref, *, tm, sub, is_meta, fc):
    d = x_ref.shape[-1]
    ext_ref[POOL_HALO:, :] = x_ref[...]
    if is_meta:
        ext_ref[:POOL_HALO, :] = jnp.zeros((POOL_HALO, d), F32)
        pos0 = 0
    else:
        j = pl.program_id(1)

        @pl.when(j == 0)
        def _():
            ext_ref[:POOL_HALO - N_META, :] = jnp.zeros((POOL_HALO - N_META, d), F32)
            ext_ref[POOL_HALO - N_META:POOL_HALO, :] = meta_ref[...]

        @pl.when(j > 0)
        def _():
            ext_ref[:POOL_HALO, :] = halo_ref[...]

        pos0 = N_META + j * tm

    def mixer(r):
        t1 = (pos0 + r + 1 + lax.broadcasted_iota(jnp.int32, (sub, 1), 0)).astype(F32)
        return _layer_norm(_pool_mix(ext_ref, lvl_ref, r, sub, t1, pw_ref, psc_ref), g1_ref[...], b1_ref[...])

    h_next = mixer(0)
    for r in range(0, tm, sub):
        h1 = h_next
        if r + sub < tm:
            h_next = mixer(r + sub)
        o_ref[r:r + sub, :] = _ffn_ln(h1, wg_ref, wu_ref, wd_ref, g2_ref, b2_ref, fc)


def _layer1_tail_kernel(a_ref, h_ref, wo_ref, g1_ref, b1_ref, wg_ref, wu_ref, wd_ref, g2_ref, b2_ref,
                        o_ref, *, tm, sub, fc):
    def mixer(r):
        m = jnp.dot(a_ref[r:r + sub, :], wo_ref[...], preferred_element_type=F32)
        return _layer_norm(DN_ALPHA * h_ref[r:r + sub, :] + m, g1_ref[...], b1_ref[...])

    h_next = mixer(0)
    for r in range(0, tm, sub):
        h = h_next
        if r + sub < tm:
            h_next = mixer(r + sub)
        o_ref[r:r + sub, :] = _ffn_ln(h, wg_ref, wu_ref, wd_ref, g2_ref, b2_ref, fc)


def _ffn_vmem_limit(tm, sub, d, f, fc, extra):
    weights = 3 * d * f * 2
    tiles = 2 * 2 * tm * d * 4
    temps = 3 * (sub * fc * (4 + 4 + 4 + 2) + 4 * sub * d * 4)
    return min(weights + tiles + temps + extra + (6 << 20), VMEM_BUDGET)


def _layer_weight_spec(shape, layer):
    return pl.BlockSpec((None,) + shape, lambda *_: (layer, 0, 0), pipeline_mode=pl.Buffered(1))


def _layer0(x, meta_tokens, meta_pad, pw, psc, g1, b1, wg, wu, wd, g2, b2, *, layer, tm, sub, fc):
    bsz, s, d = x.shape
    f = wg.shape[-1]
    ngrp = len(POOL_WINDOWS)
    cgrp = d // ngrp
    hb = tm // POOL_HALO
    assert all(w & (w - 1) == 0 and w <= MAX_WINDOW == N_META for w in POOL_WINDOWS)
    vec = _const_spec((1, d))
    consts = [_const_spec((ngrp, cgrp, cgrp)), vec, vec, vec,
              _layer_weight_spec((d, f), layer), _layer_weight_spec((d, f), layer),
              _layer_weight_spec((f, d), layer), vec, vec]
    args = (pw, psc, g1, b1, wg, wu, wd, g2, b2)
    hx = pl.pallas_call(
        functools.partial(_layer0_kernel, tm=tm, sub=sub, is_meta=False, fc=fc),
        grid=(bsz, s // tm),
        in_specs=[pl.BlockSpec((None, tm, d), lambda bi, j: (bi, j, 0)),
                  pl.BlockSpec((None, POOL_HALO, d), lambda bi, j: (bi, jnp.maximum(j * hb - 1, 0), 0)),
                  _const_spec((N_META, d))] + consts,
        out_specs=pl.BlockSpec((None, tm, d), lambda bi, j: (bi, j, 0)),
        out_shape=jax.ShapeDtypeStruct((bsz, s, d), F32),
        scratch_shapes=[pltpu.VMEM((POOL_HALO + tm, d), F32), pltpu.VMEM((ngrp, POOL_HALO + sub, cgrp), F32)],
        compiler_params=pltpu.CompilerParams(
            dimension_semantics=("arbitrary", "arbitrary"),
            vmem_limit_bytes=_ffn_vmem_limit(tm, sub, d, f, fc, (2 * POOL_HALO + tm + sub) * d * 4)),
        name="layer0_x",
    )(x, x, meta_tokens, *args)
    hm = pl.pallas_call(
        functools.partial(_layer0_kernel, tm=META_ROWS, sub=META_ROWS, is_meta=True, fc=fc),
        grid=(1,),
        in_specs=[_const_spec((META_ROWS, d)), _const_spec((N_META, d)), _const_spec((N_META, d))] + consts,
        out_specs=_const_spec((META_ROWS, d)),
        out_shape=jax.ShapeDtypeStruct((META_ROWS, d), F32),
        scratch_shapes=[pltpu.VMEM((POOL_HALO + META_ROWS, d), F32),
                        pltpu.VMEM((ngrp, POOL_HALO + META_ROWS, cgrp), F32)],
        compiler_params=pltpu.CompilerParams(
            vmem_limit_bytes=_ffn_vmem_limit(META_ROWS, META_ROWS, d, f, fc, 0)),
        name="layer0_meta",
    )(meta_pad, meta_tokens, meta_tokens, *args)
    return hx, hm


def _layer1_tail(a, h, wo, g1, b1, wg, wu, wd, g2, b2, *, layer, tm, sub, fc):
    n, d = h.shape
    f = wg.shape[-1]
    row = pl.BlockSpec((tm, d), lambda i: (i, 0))
    vec = _const_spec((1, d))
    return pl.pallas_call(
        functools.partial(_layer1_tail_kernel, tm=tm, sub=sub, fc=fc),
        grid=(n // tm,),
        in_specs=[row, row, _const_spec((d, d), True), vec, vec,
                  _layer_weight_spec((d, f), layer), _layer_weight_spec((d, f), layer),
                  _layer_weight_spec((f, d), layer), vec, vec],
        out_specs=row,
        out_shape=jax.ShapeDtypeStruct((n, d), F32),
        compiler_params=pltpu.CompilerParams(
            dimension_semantics=("arbitrary",),
            vmem_limit_bytes=_ffn_vmem_limit(tm, sub, d, f, fc, d * d * 2 + 2 * tm * d * 2 + 2 * tm * d * 4)),
        name="layer1_tail",
    )(a, h, wo, g1, b1, wg, wu, wd, g2, b2)


BIAS_PIECES = 3
VT_ROWS = 80


def _bf16_split(x):
    hi = x.astype(BF16)
    r1 = x - hi.astype(F32)
    mid = r1.astype(BF16)
    lo = (r1 - mid.astype(F32)).astype(BF16)
    return hi, mid, lo


def _bias_constants(d):
    place = np.zeros((LANES, d), np.float32)
    lane_rows = np.zeros((8, d), np.float32)
    for h in range(N_HEADS):
        base = (h // 2) * LANES + (HEAD_DIM if h % 2 == 0 else 0)
        for piece in range(BIAS_PIECES):
            neg, pos = base + piece, base + BIAS_PIECES + piece
            place[piece * N_HEADS + h, [neg, pos]] = 1.0
            lane_rows[0, pos] = 1.0
            lane_rows[1, neg] = 1.0
            lane_rows[2, neg] = -1.0
            lane_rows[3, pos] = 1.0
    lane_rows[4] = (np.arange(d) % LANES) < HEAD_DIM
    return jnp.asarray(place, BF16), jnp.asarray(lane_rows, F32)


def _inproj_kernel(h_ref, wq_ref, wk_ref, wvf_ref, bf_ref, c0_ref, place_ref, lanes_ref,
                   qe_ref, qo_ref, ke_ref, ko_ref, vt_ref, c_ref, carry_ref, *, tm):
    d = h_ref.shape[-1]

    @pl.when(pl.program_id(1) == 0)
    def _():
        carry_ref[...] = c0_ref[...]

    hb = h_ref[...].astype(BF16)
    vf = lax.dot_general(wvf_ref[...], hb, (((1,), (1,)), ((), ())), preferred_element_type=F32)
    vt = vf[:d]
    extra = jnp.where(lax.broadcasted_iota(jnp.int32, (VT_ROWS - HEAD_DIM, tm), 0) == 0, 1.0, 0.0)
    rows = []
    for h in range(N_HEADS):
        rows += [vt[h * HEAD_DIM:(h + 1) * HEAD_DIM], extra]
    vt_ref[...] = jnp.concatenate(rows, axis=0).astype(BF16)

    z = vf[d:d + N_HEADS] + bf_ref[:, 0:1]
    logf = (jnp.minimum(z, 0.0) - jnp.log1p(jnp.exp(-jnp.abs(z)))) * LOG2E
    triu = (lax.broadcasted_iota(jnp.int32, (tm, tm), 0)
            <= lax.broadcasted_iota(jnp.int32, (tm, tm), 1)).astype(BF16)
    cum3 = jnp.dot(jnp.concatenate(_bf16_split(logf), axis=0), triu, preferred_element_type=F32)
    c = (carry_ref[:, 0:1] + cum3[:N_HEADS] + cum3[N_HEADS:2 * N_HEADS] + cum3[2 * N_HEADS:])
    c_ref[...] = c
    carry_ref[...] = jnp.broadcast_to(c[:, tm - 1:tm], (N_HEADS, LANES))

    q = jnp.dot(hb, wq_ref[...], preferred_element_type=F32) * (HEAD_DIM ** -0.5 * LOG2E)

    pieces = [p.astype(F32) for p in _bf16_split(c)]
    pad = jnp.zeros((LANES - BIAS_PIECES * N_HEADS, tm), F32)
    packed = jnp.concatenate(pieces + [pad], axis=0).T
    placed = jnp.dot(packed.astype(BF16), place_ref[...], preferred_element_type=F32)
    bias_q = placed * lanes_ref[0:1, :] + lanes_ref[1:2, :]
    bias_k = placed * lanes_ref[2:3, :] + lanes_ref[3:4, :]
    first = lanes_ref[4:5, :] > 0.5

    qe_ref[...] = jnp.where(first, q, bias_q).astype(BF16)
    qo_ref[...] = jnp.where(first, bias_q, q).astype(BF16)
    k = jnp.dot(hb, wk_ref[...], preferred_element_type=F32)
    ke_ref[...] = jnp.where(first, k, bias_k).astype(BF16)
    ko_ref[...] = jnp.where(first, bias_k, k).astype(BF16)


def _inproj(h, wq, wk, wvf, bfp, c0, place, lane_rows, *, tm):
    bsz, s, d = h.shape
    row = pl.BlockSpec((None, tm, d), lambda bi, j: (bi, j, 0))
    act = jax.ShapeDtypeStruct((bsz, s, d), BF16)
    wspec = _const_spec((d, d), True)
    stat = _const_spec((N_HEADS, LANES))
    return pl.pallas_call(
        functools.partial(_inproj_kernel, tm=tm),
        grid=(bsz, s // tm),
        in_specs=[row, wspec, wspec, _const_spec((d + N_HEADS, d), True), stat, stat,
                  _const_spec((LANES, d)), _const_spec((8, d))],
        out_specs=[row, row, row, row,
                   pl.BlockSpec((None, N_HEADS * VT_ROWS, tm), lambda bi, j: (bi, 0, j)),
                   pl.BlockSpec((None, N_HEADS, tm), lambda bi, j: (bi, 0, j))],
        out_shape=[act, act, act, act,
                   jax.ShapeDtypeStruct((bsz, N_HEADS * VT_ROWS, s), BF16),
                   jax.ShapeDtypeStruct((bsz, N_HEADS, s), F32)],
        scratch_shapes=[pltpu.VMEM((N_HEADS, LANES), F32)],
        compiler_params=pltpu.CompilerParams(dimension_semantics=("arbitrary", "arbitrary"),
                                             vmem_limit_bytes=56 << 20),
        name=f"inproj_{bsz * s}",
    )(h, wq, wk, wvf, bfp, c0, place, lane_rows)


def _attn_kernel(qe_ref, qo_ref, ke_ref, ko_ref, vt_ref, kme_ref, kmo_ref, vtm_ref, o_ref,
                 s_sc, mb_sc, p_sc, pm_sc, al_sc, m_sc, acc_sc, *, tq):
    nq = o_ref.shape[0] // tq
    nh = acc_sc.shape[0]

    def rows(i):
        return pl.ds(pl.multiple_of(i * tq, tq), tq)

    def slab(g):
        return slice((g // 2) * LANES, (g // 2 + 1) * LANES)

    def q_tile(g, jq):
        return (qe_ref, qo_ref)[g % 2][rows(jq), slab(g)]

    def k_chunk(g, kc):
        return (ke_ref, ko_ref)[g % 2][rows(kc), slab(g)]

    def k_meta(g):
        return (kme_ref, kmo_ref)[g % 2][:, slab(g)]

    def vt_rows(ref, g, cols):
        return ref[g * VT_ROWS:(g + 1) * VT_ROWS, cols]

    def dot_nt(a, b):
        return lax.dot_general(a, b, (((1,), (1,)), ((), ())), preferred_element_type=F32)

    causal_t = (lax.broadcasted_iota(jnp.int32, (tq, tq), 0)
                <= lax.broadcasted_iota(jnp.int32, (tq, tq), 1))
    is_meta = lax.broadcasted_iota(jnp.int32, (META_ROWS, tq), 0) < N_META

    def put(ref, g, row):
        ref[g] = jnp.broadcast_to(row, (8, tq))

    def get(ref, g):
        return ref[g][0:1, :]

    def stage_a(jq, kc):
        for g in range(nh):
            st = dot_nt(k_chunk(g, kc), q_tile(g, jq))
            s_sc[g] = st
            put(mb_sc, g, jnp.max(st, axis=0, keepdims=True))

    def stage_b(jq, first, diag):
        for g in range(nh):
            st = s_sc[g]
            if diag:
                st = jnp.where(causal_t, st, MASK_VALUE)
                sm = jnp.where(is_meta, dot_nt(k_meta(g), q_tile(g, jq)), MASK_VALUE)
                m_blk = jnp.maximum(jnp.max(st, axis=0, keepdims=True), jnp.max(sm, axis=0, keepdims=True))
            else:
                m_blk = get(mb_sc, g)
            if first:
                m_new = m_blk
                put(al_sc, g, jnp.zeros((1, tq), F32))
            else:
                m_old = get(m_sc, g)
                m_new = jnp.maximum(m_old, m_blk)
                put(al_sc, g, jnp.exp2(m_old - m_new))
            put(m_sc, g, m_new)
            p_sc[g] = jnp.exp2(st - m_new).astype(BF16)
            if diag:
                pm_sc[g] = jnp.exp2(sm - m_new).astype(BF16)

    def stage_c(jq, kc, diag):
        outs = []
        for g in range(nh):
            pv = jnp.dot(vt_rows(vt_ref, g, rows(kc)), p_sc[g], preferred_element_type=F32)
            if diag:
                pv = pv + jnp.dot(vt_rows(vtm_ref, g, slice(None)), pm_sc[g], preferred_element_type=F32)
            acc = get(al_sc, g) * acc_sc[g] + pv
            acc_sc[g] = acc
            if diag:
                outs.append(acc[:HEAD_DIM] / acc[HEAD_DIM:HEAD_DIM + 1])
        for g in range(0, len(outs), 2):
            o_ref[rows(jq), slab(g)] = jnp.concatenate(outs[g:g + 2], axis=0).T.astype(BF16)

    def inner_chunks(jq):
        def chunk(kc):
            stage_c(jq, kc - 1, False)
            stage_b(jq, False, False)
            stage_a(jq, kc + 1)

        odd = (jq - 1) % 2

        @pl.when(odd == 1)
        def _():
            chunk(1)

        def chunk_pair(i, c):
            kc = 1 + odd + 2 * i
            chunk(kc)
            chunk(kc + 1)
            return c

        lax.fori_loop(0, (jq - 1) // 2, chunk_pair, 0)

    acc_sc[...] = jnp.zeros(acc_sc.shape, F32)
    stage_a(0, 0)
    stage_b(0, True, True)
    stage_a(1, 0)

    def tile(jq, carry):
        stage_c(jq - 1, jq - 1, True)
        stage_b(jq, True, False)
        stage_a(jq, 1)
        inner_chunks(jq)
        stage_c(jq, jq - 1, False)
        stage_b(jq, False, True)
        stage_a(jnp.minimum(jq + 1, nq - 1), 0)
        return carry

    lax.fori_loop(1, nq, tile, 0)
    stage_c(nq - 1, nq - 1, True)


def _attention(qe, qo, ke, ko, vt, kme, kmo, vtm, *, tq, npair):
    bsz, s, d = ke.shape
    nh = 2 * npair
    width = npair * LANES
    seq = pl.BlockSpec((None, s, width), lambda bi, hp: (bi, 0, hp))
    kmspec = pl.BlockSpec((META_ROWS, width), lambda bi, hp: (0, hp))
    return pl.pallas_call(
        functools.partial(_attn_kernel, tq=tq),
        grid=(bsz, d // width),
        in_specs=[seq, seq, seq, seq,
                  pl.BlockSpec((None, nh * VT_ROWS, s), lambda bi, hp: (bi, hp, 0)),
                  kmspec, kmspec,
                  pl.BlockSpec((nh * VT_ROWS, META_ROWS), lambda bi, hp: (hp, 0))],
        out_specs=seq,
        out_shape=jax.ShapeDtypeStruct((bsz, s, d), BF16),
        scratch_shapes=[pltpu.VMEM((nh, tq, tq), F32), pltpu.VMEM((nh, 8, tq), F32),
                        pltpu.VMEM((nh, tq, tq), BF16), pltpu.VMEM((nh, META_ROWS, tq), BF16),
                        pltpu.VMEM((nh, 8, tq), F32), pltpu.VMEM((nh, 8, tq), F32),
                        pltpu.VMEM((nh, VT_ROWS, tq), F32)],
        compiler_params=pltpu.CompilerParams(dimension_semantics=("arbitrary", "arbitrary"),
                                             vmem_limit_bytes=56 << 20),
        name="fox_attn",
    )(qe, qo, ke, ko, vt, kme, kmo, vtm)


def kernel(x, meta_tokens, pool_w, pool_scale, fox_w_in, fox_b_f, fox_w_o, ffn_w_gate, ffn_w_up,
           ffn_w_down, ln_g, ln_b):
    bsz, s, d = x.shape
    f_dim = ffn_w_gate.shape[-1]
    tm = 1024
    sub = 256
    tq = 512
    fc = -(-f_dim // (2 * MXU_TILE)) * MXU_TILE
    row = lambda a: a.reshape(1, -1).astype(F32)

    meta_pad = jnp.pad(meta_tokens, ((0, META_ROWS - N_META), (0, 0)))
    wg = ffn_w_gate.astype(BF16)
    wu = ffn_w_up.astype(BF16)
    wd = ffn_w_down.astype(BF16)

    h2x, h2m = _layer0(x, meta_tokens, meta_pad, pool_w[0].astype(BF16), row(pool_scale[0]),
                       row(ln_g[0, 0]), row(ln_b[0, 0]), wg, wu, wd, row(ln_g[0, 1]), row(ln_b[0, 1]),
                       layer=0, tm=tm, sub=sub, fc=fc)
    h2x = h2x.reshape(bsz * s, d)

    w_in = fox_w_in[0]
    wq, wk = (w_in[:, i * d:(i + 1) * d].astype(BF16) for i in range(2))
    wvf = w_in[:, 2 * d:].T.astype(BF16)
    bfp = jnp.broadcast_to(fox_b_f[0][:, None], (N_HEADS, LANES))
    place, lane_rows = _bias_constants(d)
    inproj = functools.partial(_inproj, wq=wq, wk=wk, wvf=wvf, bfp=bfp, place=place, lane_rows=lane_rows)
    _, _, kme, kmo, vtm, cm = inproj(h2m[None], c0=jnp.zeros((N_HEADS, LANES), F32), tm=META_ROWS)
    c0 = jnp.broadcast_to(cm[0, :, N_META - 1:N_META], (N_HEADS, LANES))
    qe, qo, ke, ko, vt, _ = inproj(h2x.reshape(bsz, s, d), c0=c0, tm=512)
    o = _attention(qe, qo, ke, ko, vt, kme[0], kmo[0], vtm[0], tq=tq, npair=2)
    out = _layer1_tail(o.reshape(bsz * s, d), h2x, fox_w_o[0].astype(BF16), row(ln_g[1, 0]), row(ln_b[1, 0]),
                       wg, wu, wd, row(ln_g[1, 1]), row(ln_b[1, 1]), layer=1, tm=tm, sub=sub, fc=fc)
    return out.reshape(bsz, s, d)
```

```python
import functools
import math

import jax
import jax.numpy as jnp
import numpy as np
from jax import lax
from jax.experimental import pallas as pl
from jax.experimental.pallas import tpu as pltpu

N_META = 16
N_HEADS = 16
HEAD_DIM = 64
POOL_WINDOWS = (2, 4, 8, 16)
MAX_WINDOW = max(POOL_WINDOWS)
DEPTH = 2
DN_ALPHA = (2.0 * DEPTH) ** 0.25
LN_EPS = 1e-5
LANES = 128
MXU_TILE = 256
VMEM_BUDGET = 60 << 20
STAGE_VMEM = 56 << 20

FFN_ROWS = 1024
FFN_SUB_ROWS = 256
INPROJ_ROWS = 1024
ATTN_TILE = 512
ATTN_SLABS = 2
META_ROWS = 128
MASK_VALUE = -1e30
LOG2E = math.log2(math.e)

F32 = jnp.float32
BF16 = jnp.bfloat16


def _layer_norm(z, g, b):
    mu = jnp.mean(z, axis=-1, keepdims=True)
    zc = z - mu
    var = jnp.mean(zc * zc, axis=-1, keepdims=True)
    return zc * lax.rsqrt(var + LN_EPS) * g + b


def _const_spec(shape, single_buffer=False):
    zeros = (0,) * len(shape)
    if single_buffer:
        return pl.BlockSpec(shape, lambda *_: zeros, pipeline_mode=pl.Buffered(1))
    return pl.BlockSpec(shape, lambda *_: zeros)


def _swiglu(hb, wg_ref, wu_ref, wd_ref, fc):
    f_dim = wg_ref.shape[1]
    out = None
    for lo in range(0, f_dim, fc):
        sl = slice(lo, min(lo + fc, f_dim))
        hg = jnp.dot(hb, wg_ref[:, sl], preferred_element_type=F32)
        hu = jnp.dot(hb, wu_ref[:, sl], preferred_element_type=F32)
        a = (jax.nn.silu(hg) * hu).astype(BF16)
        part = jnp.dot(a, wd_ref[sl, :], preferred_element_type=F32)
        out = part if out is None else out + part
    return out


def _ffn_ln(h, wg_ref, wu_ref, wd_ref, g_ref, b_ref, fc):
    f = _swiglu(h.astype(BF16), wg_ref, wu_ref, wd_ref, fc)
    return _layer_norm(DN_ALPHA * h + f, g_ref[...], b_ref[...])


POOL_HALO = 2 * MAX_WINDOW
SUBLANES = 8


def _pool_mix(ext_ref, lvl_ref, row0, rows, t1, w_ref, sc_ref):
    d = ext_ref.shape[-1]
    cgrp = d // len(POOL_WINDOWS)
    zs = []
    for g, w in enumerate(POOL_WINDOWS):
        sl = slice(g * cgrp, (g + 1) * cgrp)
        arr = ext_ref[pl.ds(row0, rows + POOL_HALO), sl]
        cur_g = arr[POOL_HALO:]
        used = 0
        for k in (1, 2, 4):
            if k >= w:
                break
            n = arr.shape[0]
            lvl_ref[g, :n, :] = arr
            arr = arr[SUBLANES:] + lvl_ref[g, pl.ds(SUBLANES - k, n - SUBLANES), :]
            used += SUBLANES
        if w > SUBLANES:
            arr = arr[SUBLANES:] + arr[:-SUBLANES]
            used += SUBLANES
        win = arr[POOL_HALO - used:]
        y = win / jnp.minimum(t1, float(w)) - cur_g
        m = jnp.dot(y.astype(BF16), w_ref[g], preferred_element_type=F32) * sc_ref[:, sl]
        zs.append(DN_ALPHA * cur_g + m)
    return jnp.concatenate(zs, axis=1)


def _layer0_kernel(x_ref, halo_ref, meta_ref, pw_ref, psc_ref, g1_ref, b1_ref, wg_ref, wu_ref, wd_ref,
                   g2_ref, b2_ref, o_ref, ext_ref, lvl_ref, *, tm, sub, is_meta, fc):
    d = x_ref.shape[-1]
    ext_ref[POOL_HALO:, :] = x_ref[...]
    if is_meta:
        ext_ref[:POOL_HALO, :] = jnp.zeros((POOL_HALO, d), F32)
        pos0 = 0
    else:
        j = pl.program_id(1)

        @pl.when(j == 0)
        def _():
            ext_ref[:POOL_HALO - N_META, :] = jnp.zeros((POOL_HALO - N_META, d), F32)
            ext_ref[POOL_HALO - N_META:POOL_HALO, :] = meta_ref[...]

        @pl.when(j > 0)
        def _():
            ext_ref[:POOL_HALO, :] = halo_ref[...]

        pos0 = N_META + j * tm

    def mixer(r):
        t1 = (pos0 + r + 1 + lax.broadcasted_iota(jnp.int32, (sub, 1), 0)).astype(F32)
        return _layer_norm(_pool_mix(ext_ref, lvl_ref, r, sub, t1, pw_ref, psc_ref), g1_ref[...], b1_ref[...])

    h_next = mixer(0)
    for r in range(0, tm, sub):
        h1 = h_next
        if r + sub < tm:
            h_next = mixer(r + sub)
        o_ref[r:r + sub, :] = _ffn_ln(h1, wg_ref, wu_ref, wd_ref, g2_ref, b2_ref, fc)


def _layer1_tail_kernel(a_ref, h_ref, wo_ref, g1_ref, b1_ref, wg_ref, wu_ref, wd_ref, g2_ref, b2_ref,
                        o_ref, *, tm, sub, fc):
    def mixer(r):
        m = jnp.dot(a_ref[r:r + sub, :], wo_ref[...], preferred_element_type=F32)
        return _layer_norm(DN_ALPHA * h_ref[r:r + sub, :] + m, g1_ref[...], b1_ref[...])

    h_next = mixer(0)
    for r in range(0, tm, sub):
        h = h_next
        if r + sub < tm:
            h_next = mixer(r + sub)
        o_ref[r:r + sub, :] = _ffn_ln(h, wg_ref, wu_ref, wd_ref, g2_ref, b2_ref, fc)


def _ffn_vmem_limit(tm, sub, d, f, fc, extra):
    weights = 3 * d * f * 2
    tiles = 2 * 2 * tm * d * 4
    temps = 3 * (sub * fc * (4 + 4 + 4 + 2) + 4 * sub * d * 4)
    return min(weights + tiles + temps + extra + (6 << 20), VMEM_BUDGET)


def _layer_weight_spec(shape, layer):
    return pl.BlockSpec((None,) + shape, lambda *_: (layer, 0, 0), pipeline_mode=pl.Buffered(1))


def _layer0(x, meta_tokens, meta_pad, pw, psc, g1, b1, wg, wu, wd, g2, b2, *, layer, tm, sub, fc):
    bsz, s, d = x.shape
    f = wg.shape[-1]
    ngrp = len(POOL_WINDOWS)
    cgrp = d // ngrp
    hb = tm // POOL_HALO
    assert all(w & (w - 1) == 0 and w <= MAX_WINDOW == N_META for w in POOL_WINDOWS)
    vec = _const_spec((1, d))
    consts = [_const_spec((ngrp, cgrp, cgrp)), vec, vec, vec,
              _layer_weight_spec((d, f), layer), _layer_weight_spec((d, f), layer),
              _layer_weight_spec((f, d), layer), vec, vec]
    args = (pw, psc, g1, b1, wg, wu, wd, g2, b2)
    hx = pl.pallas_call(
        functools.partial(_layer0_kernel, tm=tm, sub=sub, is_meta=False, fc=fc),
        grid=(bsz, s // tm),
        in_specs=[pl.BlockSpec((None, tm, d), lambda bi, j: (bi, j, 0)),
                  pl.BlockSpec((None, POOL_HALO, d), lambda bi, j: (bi, jnp.maximum(j * hb - 1, 0), 0)),
                  _const_spec((N_META, d))] + consts,
        out_specs=pl.BlockSpec((None, tm, d), lambda bi, j: (bi, j, 0)),
        out_shape=jax.ShapeDtypeStruct((bsz, s, d), F32),
        scratch_shapes=[pltpu.VMEM((POOL_HALO + tm, d), F32), pltpu.VMEM((ngrp, POOL_HALO + sub, cgrp), F32)],
        compiler_params=pltpu.CompilerParams(
            dimension_semantics=("arbitrary", "arbitrary"),
            vmem_limit_bytes=_ffn_vmem_limit(tm, sub, d, f, fc, (2 * POOL_HALO + tm + sub) * d * 4)),
        name="layer0_x",
    )(x, x, meta_tokens, *args)
    hm = pl.pallas_call(
        functools.partial(_layer0_kernel, tm=META_ROWS, sub=META_ROWS, is_meta=True, fc=fc),
        grid=(1,),
        in_specs=[_const_spec((META_ROWS, d)), _const_spec((N_META, d)), _const_spec((N_META, d))] + consts,
        out_specs=_const_spec((META_ROWS, d)),
        out_shape=jax.ShapeDtypeStruct((META_ROWS, d), F32),
        scratch_shapes=[pltpu.VMEM((POOL_HALO + META_ROWS, d), F32),
                        pltpu.VMEM((ngrp, POOL_HALO + META_ROWS, cgrp), F32)],
        compiler_params=pltpu.CompilerParams(
            vmem_limit_bytes=_ffn_vmem_limit(META_ROWS, META_ROWS, d, f, fc, 0)),
        name="layer0_meta",
    )(meta_pad, meta_tokens, meta_tokens, *args)
    return hx, hm


def _layer1_tail(a, h, wo, g1, b1, wg, wu, wd, g2, b2, *, layer, tm, sub, fc):
    n, d = h.shape
    f = wg.shape[-1]
    row = pl.BlockSpec((tm, d), lambda i: (i, 0))
    vec = _const_spec((1, d))
    return pl.pallas_call(
        functools.partial(_layer1_tail_kernel, tm=tm, sub=sub, fc=fc),
        grid=(n // tm,),
        in_specs=[row, row, _const_spec((d, d), True), vec, vec,
                  _layer_weight_spec((d, f), layer), _layer_weight_spec((d, f), layer),
                  _layer_weight_spec((f, d), layer), vec, vec],
        out_specs=row,
        out_shape=jax.ShapeDtypeStruct((n, d), F32),
        compiler_params=pltpu.CompilerParams(
            dimension_semantics=("arbitrary",),
            vmem_limit_bytes=_ffn_vmem_limit(tm, sub, d, f, fc, d * d * 2 + 2 * tm * d * 2 + 2 * tm * d * 4)),
        name="layer1_tail",
    )(a, h, wo, g1, b1, wg, wu, wd, g2, b2)


BIAS_PIECES = 3
VT_ROWS = 80


def _bf16_split(x):
    hi = x.astype(BF16)
    r1 = x - hi.astype(F32)
    mid = r1.astype(BF16)
    lo = (r1 - mid.astype(F32)).astype(BF16)
    return hi, mid, lo


def _bias_constants(d):
    place = np.zeros((LANES, d), np.float32)
    lane_rows = np.zeros((8, d), np.float32)
    for h in range(N_HEADS):
        base = (h // 2) * LANES + (HEAD_DIM if h % 2 == 0 else 0)
        for piece in range(BIAS_PIECES):
            neg, pos = base + piece, base + BIAS_PIECES + piece
            place[piece * N_HEADS + h, [neg, pos]] = 1.0
            lane_rows[0, pos] = 1.0
            lane_rows[1, neg] = 1.0
            lane_rows[2, neg] = -1.0
            lane_rows[3, pos] = 1.0
    lane_rows[4] = (np.arange(d) % LANES) < HEAD_DIM
    return jnp.asarray(place, BF16), jnp.asarray(lane_rows, F32)


def _inproj_kernel(h_ref, wq_ref, wk_ref, wvf_ref, bf_ref, c0_ref, place_ref, lanes_ref,
                   qe_ref, qo_ref, ke_ref, ko_ref, vt_ref, c_ref, carry_ref, *, tm):
    d = h_ref.shape[-1]

    @pl.when(pl.program_id(1) == 0)
    def _():
        carry_ref[...] = c0_ref[...]

    hb = h_ref[...].astype(BF16)
    vf = lax.dot_general(wvf_ref[...], hb, (((1,), (1,)), ((), ())), preferred_element_type=F32)
    vt = vf[:d]
    extra = jnp.where(lax.broadcasted_iota(jnp.int32, (VT_ROWS - HEAD_DIM, tm), 0) == 0, 1.0, 0.0)
    rows = []
    for h in range(N_HEADS):
        rows += [vt[h * HEAD_DIM:(h + 1) * HEAD_DIM], extra]
    vt_ref[...] = jnp.concatenate(rows, axis=0).astype(BF16)

    z = vf[d:d + N_HEADS] + bf_ref[:, 0:1]
    logf = (jnp.minimum(z, 0.0) - jnp.log1p(jnp.exp(-jnp.abs(z)))) * LOG2E
    triu = (lax.broadcasted_iota(jnp.int32, (tm, tm), 0)
            <= lax.broadcasted_iota(jnp.int32, (tm, tm), 1)).astype(BF16)
    cum3 = jnp.dot(jnp.concatenate(_bf16_split(logf), axis=0), triu, preferred_element_type=F32)
    c = (carry_ref[:, 0:1] + cum3[:N_HEADS] + cum3[N_HEADS:2 * N_HEADS] + cum3[2 * N_HEADS:])
    c_ref[...] = c
    carry_ref[...] = jnp.broadcast_to(c[:, tm - 1:tm], (N_HEADS, LANES))

    q = jnp.dot(hb, wq_ref[...], preferred_element_type=F32) * (HEAD_DIM ** -0.5 * LOG2E)

    pieces = [p.astype(F32) for p in _bf16_split(c)]
    pad = jnp.zeros((LANES - BIAS_PIECES * N_HEADS, tm), F32)
    packed = jnp.concatenate(pieces + [pad], axis=0).T
    placed = jnp.dot(packed.astype(BF16), place_ref[...], preferred_element_type=F32)
    bias_q = placed * lanes_ref[0:1, :] + lanes_ref[1:2, :]
    bias_k = placed * lanes_ref[2:3, :] + lanes_ref[3:4, :]
    first = lanes_ref[4:5, :] > 0.5

    qe_ref[...] = jnp.where(first, q, bias_q).astype(BF16)
    qo_ref[...] = jnp.where(first, bias_q, q).astype(BF16)
    k = jnp.dot(hb, wk_ref[...], preferred_element_type=F32)
    ke_ref[...] = jnp.where(first, k, bias_k).astype(BF16)
    ko_ref[...] = jnp.where(first, bias_k, k).astype(BF16)


def _inproj(h, wq, wk, wvf, bfp, c0, place, lane_rows, *, tm):
    bsz, s, d = h.shape
    row = pl.BlockSpec((None, tm, d), lambda bi, j: (bi, j, 0))
    act = jax.ShapeDtypeStruct((bsz, s, d), BF16)
    wspec = _const_spec((d, d), True)
    stat = _const_spec((N_HEADS, LANES))
    return pl.pallas_call(
        functools.partial(_inproj_kernel, tm=tm),
        grid=(bsz, s // tm),
        in_specs=[row, wspec, wspec, _const_spec((d + N_HEADS, d), True), stat, stat,
                  _const_spec((LANES, d)), _const_spec((8, d))],
        out_specs=[row, row, row, row,
                   pl.BlockSpec((None, N_HEADS * VT_ROWS, tm), lambda bi, j: (bi, 0, j)),
                   pl.BlockSpec((None, N_HEADS, tm), lambda bi, j: (bi, 0, j))],
        out_shape=[act, act, act, act,
                   jax.ShapeDtypeStruct((bsz, N_HEADS * VT_ROWS, s), BF16),
                   jax.ShapeDtypeStruct((bsz, N_HEADS, s), F32)],
        scratch_shapes=[pltpu.VMEM((N_HEADS, LANES), F32)],
        compiler_params=pltpu.CompilerParams(dimension_semantics=("arbitrary", "arbitrary"),
                                             vmem_limit_bytes=STAGE_VMEM),
        name=f"inproj_{bsz * s}",
    )(h, wq, wk, wvf, bfp, c0, place, lane_rows)


def _attn_kernel(qe_ref, qo_ref, ke_ref, ko_ref, vt_ref, kme_ref, kmo_ref, vtm_ref, o_ref,
                 s_sc, mb_sc, p_sc, pm_sc, al_sc, m_sc, acc_sc, *, tq):
    nq = o_ref.shape[0] // tq
    nh = acc_sc.shape[0]

    def rows(i):
        return pl.ds(pl.multiple_of(i * tq, tq), tq)

    def slab(g):
        return slice((g // 2) * LANES, (g // 2 + 1) * LANES)

    def q_tile(g, jq):
        return (qe_ref, qo_ref)[g % 2][rows(jq), slab(g)]

    def k_chunk(g, kc):
        return (ke_ref, ko_ref)[g % 2][rows(kc), slab(g)]

    def k_meta(g):
        return (kme_ref, kmo_ref)[g % 2][:, slab(g)]

    def vt_rows(ref, g, cols):
        return ref[g * VT_ROWS:(g + 1) * VT_ROWS, cols]

    def dot_nt(a, b):
        return lax.dot_general(a, b, (((1,), (1,)), ((), ())), preferred_element_type=F32)

    causal_t = (lax.broadcasted_iota(jnp.int32, (tq, tq), 0)
                <= lax.broadcasted_iota(jnp.int32, (tq, tq), 1))
    is_meta = lax.broadcasted_iota(jnp.int32, (META_ROWS, tq), 0) < N_META

    def put(ref, g, row):
        ref[g] = jnp.broadcast_to(row, (8, tq))

    def get(ref, g):
        return ref[g][0:1, :]

    def stage_a(jq, kc):
        for g in range(nh):
            st = dot_nt(k_chunk(g, kc), q_tile(g, jq))
            s_sc[g] = st
            put(mb_sc, g, jnp.max(st, axis=0, keepdims=True))

    def stage_b(jq, first, diag):
        for g in range(nh):
            st = s_sc[g]
            if diag:
                st = jnp.where(causal_t, st, MASK_VALUE)
                sm = jnp.where(is_meta, dot_nt(k_meta(g), q_tile(g, jq)), MASK_VALUE)
                m_blk = jnp.maximum(jnp.max(st, axis=0, keepdims=True), jnp.max(sm, axis=0, keepdims=True))
            else:
                m_blk = get(mb_sc, g)
            if first:
                m_new = m_blk
                put(al_sc, g, jnp.zeros((1, tq), F32))
            else:
                m_old = get(m_sc, g)
                m_new = jnp.maximum(m_old, m_blk)
                put(al_sc, g, jnp.exp2(m_old - m_new))
            put(m_sc, g, m_new)
            p_sc[g] = jnp.exp2(st - m_new).astype(BF16)
            if diag:
                pm_sc[g] = jnp.exp2(sm - m_new).astype(BF16)

    def stage_c(jq, kc, diag):
        outs = []
        for g in range(nh):
            pv = jnp.dot(vt_rows(vt_ref, g, rows(kc)), p_sc[g], preferred_element_type=F32)
            if diag:
                pv = pv + jnp.dot(vt_rows(vtm_ref, g, slice(None)), pm_sc[g], preferred_element_type=F32)
            acc = get(al_sc, g) * acc_sc[g] + pv
            acc_sc[g] = acc
            if diag:
                outs.append(acc[:HEAD_DIM] / acc[HEAD_DIM:HEAD_DIM + 1])
        for g in range(0, len(outs), 2):
            o_ref[rows(jq), slab(g)] = jnp.concatenate(outs[g:g + 2], axis=0).T.astype(BF16)

    def inner_chunks(jq):
        def chunk(kc):
            stage_c(jq, kc - 1, False)
            stage_b(jq, False, False)
            stage_a(jq, kc + 1)

        odd = (jq - 1) % 2

        @pl.when(odd == 1)
        def _():
            chunk(1)

        def chunk_pair(i, c):
            kc = 1 + odd + 2 * i
            chunk(kc)
            chunk(kc + 1)
            return c

        lax.fori_loop(0, (jq - 1) // 2, chunk_pair, 0)

    acc_sc[...] = jnp.zeros(acc_sc.shape, F32)
    stage_a(0, 0)
    stage_b(0, True, True)
    stage_a(1, 0)

    def tile(jq, carry):
        stage_c(jq - 1, jq - 1, True)
        stage_b(jq, True, False)
        stage_a(jq, 1)
        inner_chunks(jq)
        stage_c(jq, jq - 1, False)
        stage_b(jq, False, True)
        stage_a(jnp.minimum(jq + 1, nq - 1), 0)
        return carry

    lax.fori_loop(1, nq, tile, 0)
    stage_c(nq - 1, nq - 1, True)


def _attention(qe, qo, ke, ko, vt, kme, kmo, vtm, *, tq, npair):
    bsz, s, d = ke.shape
    nh = 2 * npair
    width = npair * LANES
    seq = pl.BlockSpec((None, s, width), lambda bi, hp: (bi, 0, hp))
    kmspec = pl.BlockSpec((META_ROWS, width), lambda bi, hp: (0, hp))
    return pl.pallas_call(
        functools.partial(_attn_kernel, tq=tq),
        grid=(bsz, d // width),
        in_specs=[seq, seq, seq, seq,
                  pl.BlockSpec((None, nh * VT_ROWS, s), lambda bi, hp: (bi, hp, 0)),
                  kmspec, kmspec,
                  pl.BlockSpec((nh * VT_ROWS, META_ROWS), lambda bi, hp: (hp, 0))],
        out_specs=seq,
        out_shape=jax.ShapeDtypeStruct((bsz, s, d), BF16),
        scratch_shapes=[pltpu.VMEM((nh, tq, tq), F32), pltpu.VMEM((nh, 8, tq), F32),
                        pltpu.VMEM((nh, tq, tq), BF16), pltpu.VMEM((nh, META_ROWS, tq), BF16),
                        pltpu.VMEM((nh, 8, tq), F32), pltpu.VMEM((nh, 8, tq), F32),
                        pltpu.VMEM((nh, VT_ROWS, tq), F32)],
        compiler_params=pltpu.CompilerParams(dimension_semantics=("arbitrary", "arbitrary"),
                                             vmem_limit_bytes=STAGE_VMEM),
        name="fox_attn",
    )(qe, qo, ke, ko, vt, kme, kmo, vtm)


def kernel(x, meta_tokens, pool_w, pool_scale, fox_w_in, fox_b_f, fox_w_o, ffn_w_gate, ffn_w_up,
           ffn_w_down, ln_g, ln_b):
    bsz, s, d = x.shape
    f_dim = ffn_w_gate.shape[-1]
    tm, sub = FFN_ROWS, FFN_SUB_ROWS
    fc = -(-f_dim // (2 * MXU_TILE)) * MXU_TILE
    row = lambda a: a.reshape(1, -1).astype(F32)

    meta_pad = jnp.pad(meta_tokens, ((0, META_ROWS - N_META), (0, 0)))
    wg = ffn_w_gate.astype(BF16)
    wu = ffn_w_up.astype(BF16)
    wd = ffn_w_down.astype(BF16)

    h2x, h2m = _layer0(x, meta_tokens, meta_pad, pool_w[0].astype(BF16), row(pool_scale[0]),
                       row(ln_g[0, 0]), row(ln_b[0, 0]), wg, wu, wd, row(ln_g[0, 1]), row(ln_b[0, 1]),
                       layer=0, tm=tm, sub=sub, fc=fc)
    h2x = h2x.reshape(bsz * s, d)

    w_in = fox_w_in[0]
    wq, wk = (w_in[:, i * d:(i + 1) * d].astype(BF16) for i in range(2))
    wvf = w_in[:, 2 * d:].T.astype(BF16)
    bfp = jnp.broadcast_to(fox_b_f[0][:, None], (N_HEADS, LANES))
    place, lane_rows = _bias_constants(d)
    inproj = functools.partial(_inproj, wq=wq, wk=wk, wvf=wvf, bfp=bfp, place=place, lane_rows=lane_rows)
    _, _, kme, kmo, vtm, cm = inproj(h2m[None], c0=jnp.zeros((N_HEADS, LANES), F32), tm=META_ROWS)
    c0 = jnp.broadcast_to(cm[0, :, N_META - 1:N_META], (N_HEADS, LANES))
    qe, qo, ke, ko, vt, _ = inproj(h2x.reshape(bsz, s, d), c0=c0, tm=INPROJ_ROWS)
    o = _attention(qe, qo, ke, ko, vt, kme[0], kmo[0], vtm[0], tq=ATTN_TILE, npair=ATTN_SLABS)
    out = _layer1_tail(o.reshape(bsz * s, d), h2x, fox_w_o[0].astype(BF16), row(ln_g[1, 0]), row(ln_b[1, 0]),
                       wg, wu, wd, row(ln_g[1, 1]), row(ln_b[1, 1]), layer=1, tm=tm, sub=sub, fc=fc)
    return out.reshape(bsz, s, d)
```

```python
import functools
import math

import jax
import jax.numpy as jnp
import numpy as np
from jax import lax
from jax.experimental import pallas as pl
from jax.experimental.pallas import tpu as pltpu

N_META = 16
N_HEADS = 16
HEAD_DIM = 64
POOL_WINDOWS = (2, 4, 8, 16)
MAX_WINDOW = max(POOL_WINDOWS)
DEPTH = 2
DN_ALPHA = (2.0 * DEPTH) ** 0.25
LN_EPS = 1e-5
LANES = 128
MXU_TILE = 256
VMEM_BUDGET = 60 << 20
STAGE_VMEM = 56 << 20

FFN_ROWS = 1024
FFN_SUB_ROWS = 256
INPROJ_ROWS = 1024
ATTN_TILE = 512
ATTN_SLABS = 2
META_ROWS = 128
MASK_VALUE = -1e30
LOG2E = math.log2(math.e)

F32 = jnp.float32
BF16 = jnp.bfloat16


def _layer_norm(z, g, b):
    mu = jnp.mean(z, axis=-1, keepdims=True)
    zc = z - mu
    var = jnp.mean(zc * zc, axis=-1, keepdims=True)
    return zc * lax.rsqrt(var + LN_EPS) * g + b


def _const_spec(shape, single_buffer=False):
    zeros = (0,) * len(shape)
    if single_buffer:
        return pl.BlockSpec(shape, lambda *_: zeros, pipeline_mode=pl.Buffered(1))
    return pl.BlockSpec(shape, lambda *_: zeros)


def _swiglu(hb, wg_ref, wu_ref, wd_ref, fc):
    f_dim = wg_ref.shape[1]
    out = None
    for lo in range(0, f_dim, fc):
        sl = slice(lo, min(lo + fc, f_dim))
        hg = jnp.dot(hb, wg_ref[:, sl], preferred_element_type=F32)
        hu = jnp.dot(hb, wu_ref[:, sl], preferred_element_type=F32)
        a = (jax.nn.silu(hg) * hu).astype(BF16)
        part = jnp.dot(a, wd_ref[sl, :], preferred_element_type=F32)
        out = part if out is None else out + part
    return out


def _ffn_ln(h, wg_ref, wu_ref, wd_ref, g_ref, b_ref, fc):
    f = _swiglu(h.astype(BF16), wg_ref, wu_ref, wd_ref, fc)
    return _layer_norm(DN_ALPHA * h + f, g_ref[...], b_ref[...])


POOL_HALO = 2 * MAX_WINDOW
SUBLANES = 8


def _pool_mix(ext_ref, lvl_ref, row0, rows, t1, w_ref, sc_ref):
    d = ext_ref.shape[-1]
    cgrp = d // len(POOL_WINDOWS)
    zs = []
    for g, w in enumerate(POOL_WINDOWS):
        sl = slice(g * cgrp, (g + 1) * cgrp)
        arr = ext_ref[pl.ds(row0, rows + POOL_HALO), sl]
        cur_g = arr[POOL_HALO:]
        used = 0
        for k in (1, 2, 4):
            if k >= w:
                break
            n = arr.shape[0]
            lvl_ref[g, :n, :] = arr
            arr = arr[SUBLANES:] + lvl_ref[g, pl.ds(SUBLANES - k, n - SUBLANES), :]
            used += SUBLANES
        if w > SUBLANES:
            arr = arr[SUBLANES:] + arr[:-SUBLANES]
            used += SUBLANES
        win = arr[POOL_HALO - used:]
        y = win / jnp.minimum(t1, float(w)) - cur_g
        m = jnp.dot(y.astype(BF16), w_ref[g], preferred_element_type=F32) * sc_ref[:, sl]
        zs.append(DN_ALPHA * cur_g + m)
    return jnp.concatenate(zs, axis=1)


def _layer0_kernel(x_ref, halo_ref, meta_ref, pw_ref, psc_ref, g1_ref, b1_ref, wg_ref, wu_ref, wd_ref,
                   g2_ref, b2_ref, o_ref, ext_ref, lvl_ref, *, tm, sub, is_meta, fc):
    d = x_ref.shape[-1]
    ext_ref[POOL_HALO:, :] = x_ref[...]
    if is_meta:
        ext_ref[:POOL_HALO, :] = jnp.zeros((POOL_HALO, d), F32)
        pos0 = 0
    else:
        j = pl.program_id(1)

        @pl.when(j == 0)
        def _():
            ext_ref[:POOL_HALO - N_META, :] = jnp.zeros((POOL_HALO - N_META, d), F32)
            ext_ref[POOL_HALO - N_META:POOL_HALO, :] = meta_ref[...]

        @pl.when(j > 0)
        def _():
            ext_ref[:POOL_HALO, :] = halo_ref[...]

        pos0 = N_META + j * tm

    def mixer(r):
        t1 = (pos0 + r + 1 + lax.broadcasted_iota(jnp.int32, (sub, 1), 0)).astype(F32)
        return _layer_norm(_pool_mix(ext_ref, lvl_ref, r, sub, t1, pw_ref, psc_ref), g1_ref[...], b1_ref[...])

    h_next = mixer(0)
    for r in range(0, tm, sub):
        h1 = h_next
        if r + sub < tm:
            h_next = mixer(r + sub)
        o_ref[r:r + sub, :] = _ffn_ln(h1, wg_ref, wu_ref, wd_ref, g2_ref, b2_ref, fc)


def _layer1_tail_kernel(a_ref, h_ref, wo_ref, g1_ref, b1_ref, wg_ref, wu_ref, wd_ref, g2_ref, b2_ref,
                        o_ref, *, tm, sub, fc):
    def mixer(r):
        m = jnp.dot(a_ref[r:r + sub, :], wo_ref[...], preferred_element_type=F32)
        return _layer_norm(DN_ALPHA * h_ref[r:r + sub, :] + m, g1_ref[...], b1_ref[...])

    h_next = mixer(0)
    for r in range(0, tm, sub):
        h = h_next
        if r + sub < tm:
            h_next = mixer(r + sub)
        o_ref[r:r + sub, :] = _ffn_ln(h, wg_ref, wu_ref, wd_ref, g2_ref, b2_ref, fc)


def _ffn_vmem_limit(tm, sub, d, f, fc, extra):
    weights = 3 * d * f * 2
    tiles = 2 * 2 * tm * d * 4
    temps = 3 * (sub * fc * (4 + 4 + 4 + 2) + 4 * sub * d * 4)
    return min(weights + tiles + temps + extra + (6 << 20), VMEM_BUDGET)


def _layer_weight_spec(shape, layer):
    return pl.BlockSpec((None,) + shape, lambda *_: (layer, 0, 0), pipeline_mode=pl.Buffered(1))


def _layer0(x, meta_tokens, meta_pad, pw, psc, g1, b1, wg, wu, wd, g2, b2, *, layer, tm, sub, fc):
    bsz, s, d = x.shape
    f = wg.shape[-1]
    ngrp = len(POOL_WINDOWS)
    cgrp = d // ngrp
    hb = tm // POOL_HALO
    assert all(w & (w - 1) == 0 and w <= MAX_WINDOW == N_META for w in POOL_WINDOWS)
    vec = _const_spec((1, d))
    consts = [_const_spec((ngrp, cgrp, cgrp)), vec, vec, vec,
              _layer_weight_spec((d, f), layer), _layer_weight_spec((d, f), layer),
              _layer_weight_spec((f, d), layer), vec, vec]
    args = (pw, psc, g1, b1, wg, wu, wd, g2, b2)
    hx = pl.pallas_call(
        functools.partial(_layer0_kernel, tm=tm, sub=sub, is_meta=False, fc=fc),
        grid=(bsz, s // tm),
        in_specs=[pl.BlockSpec((None, tm, d), lambda bi, j: (bi, j, 0)),
                  pl.BlockSpec((None, POOL_HALO, d), lambda bi, j: (bi, jnp.maximum(j * hb - 1, 0), 0)),
                  _const_spec((N_META, d))] + consts,
        out_specs=pl.BlockSpec((None, tm, d), lambda bi, j: (bi, j, 0)),
        out_shape=jax.ShapeDtypeStruct((bsz, s, d), F32),
        scratch_shapes=[pltpu.VMEM((POOL_HALO + tm, d), F32), pltpu.VMEM((ngrp, POOL_HALO + sub, cgrp), F32)],
        compiler_params=pltpu.CompilerParams(
            dimension_semantics=("arbitrary", "arbitrary"),
            vmem_limit_bytes=_ffn_vmem_limit(tm, sub, d, f, fc, (2 * POOL_HALO + tm + sub) * d * 4)),
        name="layer0_x",
    )(x, x, meta_tokens, *args)
    hm = pl.pallas_call(
        functools.partial(_layer0_kernel, tm=META_ROWS, sub=META_ROWS, is_meta=True, fc=fc),
        grid=(1,),
        in_specs=[_const_spec((META_ROWS, d)), _const_spec((N_META, d)), _const_spec((N_META, d))] + consts,
        out_specs=_const_spec((META_ROWS, d)),
        out_shape=jax.ShapeDtypeStruct((META_ROWS, d), F32),
        scratch_shapes=[pltpu.VMEM((POOL_HALO + META_ROWS, d), F32),
                        pltpu.VMEM((ngrp, POOL_HALO + META_ROWS, cgrp), F32)],
        compiler_params=pltpu.CompilerParams(
            vmem_limit_bytes=_ffn_vmem_limit(META_ROWS, META_ROWS, d, f, fc, 0)),
        name="layer0_meta",
    )(meta_pad, meta_tokens, meta_tokens, *args)
    return hx, hm


def _layer1_tail(a, h, wo, g1, b1, wg, wu, wd, g2, b2, *, layer, tm, sub, fc):
    n, d = h.shape
    f = wg.shape[-1]
    row = pl.BlockSpec((tm, d), lambda i: (i, 0))
    vec = _const_spec((1, d))
    return pl.pallas_call(
        functools.partial(_layer1_tail_kernel, tm=tm, sub=sub, fc=fc),
        grid=(n // tm,),
        in_specs=[row, row, _const_spec((d, d), True), vec, vec,
                  _layer_weight_spec((d, f), layer), _layer_weight_spec((d, f), layer),
                  _layer_weight_spec((f, d), layer), vec, vec],
        out_specs=row,
        out_shape=jax.ShapeDtypeStruct((n, d), F32),
        compiler_params=pltpu.CompilerParams(
            dimension_semantics=("arbitrary",),
            vmem_limit_bytes=_ffn_vmem_limit(tm, sub, d, f, fc, d * d * 2 + 2 * tm * d * 2 + 2 * tm * d * 4)),
        name="layer1_tail",
    )(a, h, wo, g1, b1, wg, wu, wd, g2, b2)


BIAS_PIECES = 3
VT_ROWS = 80


def _bf16_split(x):
    hi = x.astype(BF16)
    r1 = x - hi.astype(F32)
    mid = r1.astype(BF16)
    lo = (r1 - mid.astype(F32)).astype(BF16)
    return hi, mid, lo


def _bias_constants(d):
    place = np.zeros((LANES, d), np.float32)
    lane_rows = np.zeros((8, d), np.float32)
    for h in range(N_HEADS):
        base = (h // 2) * LANES + (HEAD_DIM if h % 2 == 0 else 0)
        for piece in range(BIAS_PIECES):
            neg, pos = base + piece, base + BIAS_PIECES + piece
            place[piece * N_HEADS + h, [neg, pos]] = 1.0
            lane_rows[0, pos] = 1.0
            lane_rows[1, neg] = 1.0
            lane_rows[2, neg] = -1.0
            lane_rows[3, pos] = 1.0
    lane_rows[4] = (np.arange(d) % LANES) < HEAD_DIM
    return jnp.asarray(place, BF16), jnp.asarray(lane_rows, F32)


def _inproj_kernel(h_ref, wq_ref, wk_ref, wvf_ref, bf_ref, c0_ref, place_ref, lanes_ref,
                   qe_ref, qo_ref, ke_ref, ko_ref, vt_ref, c_ref, carry_ref, *, tm):
    d = h_ref.shape[-1]

    @pl.when(pl.program_id(1) == 0)
    def _():
        carry_ref[...] = c0_ref[...]

    hb = h_ref[...].astype(BF16)
    vf = lax.dot_general(wvf_ref[...], hb, (((1,), (1,)), ((), ())), preferred_element_type=F32)
    vt = vf[:d]
    extra = jnp.where(lax.broadcasted_iota(jnp.int32, (VT_ROWS - HEAD_DIM, tm), 0) == 0, 1.0, 0.0)
    rows = []
    for h in range(N_HEADS):
        rows += [vt[h * HEAD_DIM:(h + 1) * HEAD_DIM], extra]
    vt_ref[...] = jnp.concatenate(rows, axis=0).astype(BF16)

    z = vf[d:d + N_HEADS] + bf_ref[:, 0:1]
    logf = (jnp.minimum(z, 0.0) - jnp.log1p(jnp.exp(-jnp.abs(z)))) * LOG2E
    triu = (lax.broadcasted_iota(jnp.int32, (tm, tm), 0)
            <= lax.broadcasted_iota(jnp.int32, (tm, tm), 1)).astype(BF16)
    cum3 = jnp.dot(jnp.concatenate(_bf16_split(logf), axis=0), triu, preferred_element_type=F32)
    c = (carry_ref[:, 0:1] + cum3[:N_HEADS] + cum3[N_HEADS:2 * N_HEADS] + cum3[2 * N_HEADS:])
    c_ref[...] = c
    carry_ref[...] = jnp.broadcast_to(c[:, tm - 1:tm], (N_HEADS, LANES))

    q = jnp.dot(hb, wq_ref[...], preferred_element_type=F32) * (HEAD_DIM ** -0.5 * LOG2E)

    pieces = [p.astype(F32) for p in _bf16_split(c)]
    pad = jnp.zeros((LANES - BIAS_PIECES * N_HEADS, tm), F32)
    packed = jnp.concatenate(pieces + [pad], axis=0).T
    placed = jnp.dot(packed.astype(BF16), place_ref[...], preferred_element_type=F32)
    bias_q = placed * lanes_ref[0:1, :] + lanes_ref[1:2, :]
    bias_k = placed * lanes_ref[2:3, :] + lanes_ref[3:4, :]
    first = lanes_ref[4:5, :] > 0.5

    qe_ref[...] = jnp.where(first, q, bias_q).astype(BF16)
    qo_ref[...] = jnp.where(first, bias_q, q).astype(BF16)
    k = jnp.dot(hb, wk_ref[...], preferred_element_type=F32)
    ke_ref[...] = jnp.where(first, k, bias_k).astype(BF16)
    ko_ref[...] = jnp.where(first, bias_k, k).astype(BF16)


def _inproj(h, wq, wk, wvf, bfp, c0, place, lane_rows, *, tm):
    bsz, s, d = h.shape
    row = pl.BlockSpec((None, tm, d), lambda bi, j: (bi, j, 0))
    act = jax.ShapeDtypeStruct((bsz, s, d), BF16)
    wspec = _const_spec((d, d), True)
    stat = _const_spec((N_HEADS, LANES))
    return pl.pallas_call(
        functools.partial(_inproj_kernel, tm=tm),
        grid=(bsz, s // tm),
        in_specs=[row, wspec, wspec, _const_spec((d + N_HEADS, d), True), stat, stat,
                  _const_spec((LANES, d)), _const_spec((8, d))],
        out_specs=[row, row, row, row,
                   pl.BlockSpec((None, N_HEADS * VT_ROWS, tm), lambda bi, j: (bi, 0, j)),
                   pl.BlockSpec((None, N_HEADS, tm), lambda bi, j: (bi, 0, j))],
        out_shape=[act, act, act, act,
                   jax.ShapeDtypeStruct((bsz, N_HEADS * VT_ROWS, s), BF16),
                   jax.ShapeDtypeStruct((bsz, N_HEADS, s), F32)],
        scratch_shapes=[pltpu.VMEM((N_HEADS, LANES), F32)],
        compiler_params=pltpu.CompilerParams(dimension_semantics=("arbitrary", "arbitrary"),
                                             vmem_limit_bytes=STAGE_VMEM),
        name=f"inproj_{bsz * s}",
    )(h, wq, wk, wvf, bfp, c0, place, lane_rows)


def _attn_kernel(qe_ref, qo_ref, ke_ref, ko_ref, vt_ref, kme_ref, kmo_ref, vtm_ref, o_ref,
                 s_sc, mb_sc, p_sc, pm_sc, al_sc, m_sc, acc_sc, *, tq):
    nq = o_ref.shape[0] // tq
    nh = acc_sc.shape[0]

    def rows(i):
        return pl.ds(pl.multiple_of(i * tq, tq), tq)

    def slab(g):
        return slice((g // 2) * LANES, (g // 2 + 1) * LANES)

    def q_tile(g, jq):
        return (qe_ref, qo_ref)[g % 2][rows(jq), slab(g)]

    def k_chunk(g, kc):
        return (ke_ref, ko_ref)[g % 2][rows(kc), slab(g)]

    def k_meta(g):
        return (kme_ref, kmo_ref)[g % 2][:N_META, slab(g)]

    def vt_rows(ref, g, cols):
        return ref[g * VT_ROWS:(g + 1) * VT_ROWS, cols]

    def dot_nt(a, b):
        return lax.dot_general(a, b, (((1,), (1,)), ((), ())), preferred_element_type=F32)

    causal_t = (lax.broadcasted_iota(jnp.int32, (tq, tq), 0)
                <= lax.broadcasted_iota(jnp.int32, (tq, tq), 1))

    def put(ref, g, row):
        ref[g] = jnp.broadcast_to(row, (8, tq))

    def get(ref, g):
        return ref[g][0:1, :]

    def stage_a(jq, kc):
        for g in range(nh):
            st = dot_nt(k_chunk(g, kc), q_tile(g, jq))
            s_sc[g] = st
            put(mb_sc, g, jnp.max(st, axis=0, keepdims=True))

    def stage_b(jq, first, diag):
        for g in range(nh):
            st = s_sc[g]
            if diag:
                st = jnp.where(causal_t, st, MASK_VALUE)
                sm = dot_nt(k_meta(g), q_tile(g, jq))
                m_blk = jnp.maximum(jnp.max(st, axis=0, keepdims=True), jnp.max(sm, axis=0, keepdims=True))
            else:
                m_blk = get(mb_sc, g)
            if first:
                m_new = m_blk
                put(al_sc, g, jnp.zeros((1, tq), F32))
            else:
                m_old = get(m_sc, g)
                m_new = jnp.maximum(m_old, m_blk)
                put(al_sc, g, jnp.exp2(m_old - m_new))
            put(m_sc, g, m_new)
            p_sc[g] = jnp.exp2(st - m_new).astype(BF16)
            if diag:
                pm_sc[g, :N_META, :] = jnp.exp2(sm - m_new).astype(BF16)

    def stage_c(jq, kc, diag):
        outs = []
        for g in range(nh):
            pv = jnp.dot(vt_rows(vt_ref, g, rows(kc)), p_sc[g], preferred_element_type=F32)
            if diag:
                pv = pv + jnp.dot(vt_rows(vtm_ref, g, slice(None)), pm_sc[g], preferred_element_type=F32)
            acc = get(al_sc, g) * acc_sc[g] + pv
            acc_sc[g] = acc
            if diag:
                outs.append(acc[:HEAD_DIM] / acc[HEAD_DIM:HEAD_DIM + 1])
        for g in range(0, len(outs), 2):
            o_ref[rows(jq), slab(g)] = jnp.concatenate(outs[g:g + 2], axis=0).T.astype(BF16)

    def inner_chunks(jq):
        def chunk(kc):
            stage_c(jq, kc - 1, False)
            stage_b(jq, False, False)
            stage_a(jq, kc + 1)

        odd = (jq - 1) % 2

        @pl.when(odd == 1)
        def _():
            chunk(1)

        def chunk_pair(i, c):
            kc = 1 + odd + 2 * i
            chunk(kc)
            chunk(kc + 1)
            return c

        lax.fori_loop(0, (jq - 1) // 2, chunk_pair, 0)

    acc_sc[...] = jnp.zeros(acc_sc.shape, F32)
    pm_sc[...] = jnp.zeros(pm_sc.shape, BF16)
    stage_a(0, 0)
    stage_b(0, True, True)
    stage_a(1, 0)

    def tile(jq, carry):
        stage_c(jq - 1, jq - 1, True)
        stage_b(jq, True, False)
        stage_a(jq, 1)
        inner_chunks(jq)
        stage_c(jq, jq - 1, False)
        stage_b(jq, False, True)
        stage_a(jnp.minimum(jq + 1, nq - 1), 0)
        return carry

    lax.fori_loop(1, nq, tile, 0)
    stage_c(nq - 1, nq - 1, True)


def _attention(qe, qo, ke, ko, vt, kme, kmo, vtm, *, tq, npair):
    bsz, s, d = ke.shape
    nh = 2 * npair
    width = npair * LANES
    seq = pl.BlockSpec((None, s, width), lambda bi, hp: (bi, 0, hp))
    kmspec = pl.BlockSpec((META_ROWS, width), lambda bi, hp: (0, hp))
    return pl.pallas_call(
        functools.partial(_attn_kernel, tq=tq),
        grid=(bsz, d // width),
        in_specs=[seq, seq, seq, seq,
                  pl.BlockSpec((None, nh * VT_ROWS, s), lambda bi, hp: (bi, hp, 0)),
                  kmspec, kmspec,
                  pl.BlockSpec((nh * VT_ROWS, META_ROWS), lambda bi, hp: (hp, 0))],
        out_specs=seq,
        out_shape=jax.ShapeDtypeStruct((bsz, s, d), BF16),
        scratch_shapes=[pltpu.VMEM((nh, tq, tq), F32), pltpu.VMEM((nh, 8, tq), F32),
                        pltpu.VMEM((nh, tq, tq), BF16), pltpu.VMEM((nh, META_ROWS, tq), BF16),
                        pltpu.VMEM((nh, 8, tq), F32), pltpu.VMEM((nh, 8, tq), F32),
                        pltpu.VMEM((nh, VT_ROWS, tq), F32)],
        compiler_params=pltpu.CompilerParams(dimension_semantics=("arbitrary", "arbitrary"),
                                             vmem_limit_bytes=STAGE_VMEM),
        name="fox_attn",
    )(qe, qo, ke, ko, vt, kme, kmo, vtm)


def kernel(x, meta_tokens, pool_w, pool_scale, fox_w_in, fox_b_f, fox_w_o, ffn_w_gate, ffn_w_up,
           ffn_w_down, ln_g, ln_b):
    bsz, s, d = x.shape
    f_dim = ffn_w_gate.shape[-1]
    tm, sub = FFN_ROWS, FFN_SUB_ROWS
    fc = -(-f_dim // (2 * MXU_TILE)) * MXU_TILE
    row = lambda a: a.reshape(1, -1).astype(F32)

    meta_pad = jnp.pad(meta_tokens, ((0, META_ROWS - N_META), (0, 0)))
    wg = ffn_w_gate.astype(BF16)
    wu = ffn_w_up.astype(BF16)
    wd = ffn_w_down.astype(BF16)

    h2x, h2m = _layer0(x, meta_tokens, meta_pad, pool_w[0].astype(BF16), row(pool_scale[0]),
                       row(ln_g[0, 0]), row(ln_b[0, 0]), wg, wu, wd, row(ln_g[0, 1]), row(ln_b[0, 1]),
                       layer=0, tm=tm, sub=sub, fc=fc)
    h2x = h2x.reshape(bsz * s, d)

    w_in = fox_w_in[0]
    wq, wk = (w_in[:, i * d:(i + 1) * d].astype(BF16) for i in range(2))
    wvf = w_in[:, 2 * d:].T.astype(BF16)
    bfp = jnp.broadcast_to(fox_b_f[0][:, None], (N_HEADS, LANES))
    place, lane_rows = _bias_constants(d)
    inproj = functools.partial(_inproj, wq=wq, wk=wk, wvf=wvf, bfp=bfp, place=place, lane_rows=lane_rows)
    _, _, kme, kmo, vtm, cm = inproj(h2m[None], c0=jnp.zeros((N_HEADS, LANES), F32), tm=META_ROWS)
    c0 = jnp.broadcast_to(cm[0, :, N_META - 1:N_META], (N_HEADS, LANES))
    qe, qo, ke, ko, vt, _ = inproj(h2x.reshape(bsz, s, d), c0=c0, tm=INPROJ_ROWS)
    o = _attention(qe, qo, ke, ko, vt, kme[0], kmo[0], vtm[0], tq=ATTN_TILE, npair=ATTN_SLABS)
    out = _layer1_tail(o.reshape(bsz * s, d), h2x, fox_w_o[0].astype(BF16), row(ln_g[1, 0]), row(ln_b[1, 0]),
                       wg, wu, wd, row(ln_g[1, 1]), row(ln_b[1, 1]), layer=1, tm=tm, sub=sub, fc=fc)
    return out.reshape(bsz, s, d)
```

```python
import functools
import math

import jax
import jax.numpy as jnp
import numpy as np
from jax import lax
from jax.experimental import pallas as pl
from jax.experimental.pallas import tpu as pltpu

N_META = 16
N_HEADS = 16
HEAD_DIM = 64
POOL_WINDOWS = (2, 4, 8, 16)
MAX_WINDOW = max(POOL_WINDOWS)
DEPTH = 2
DN_ALPHA = (2.0 * DEPTH) ** 0.25
LN_EPS = 1e-5
LANES = 128
MXU_TILE = 256
VMEM_BUDGET = 60 << 20
STAGE_VMEM = 56 << 20

FFN_ROWS = 1024
FFN_SUB_ROWS = 256
INPROJ_ROWS = 1024
ATTN_TILE = 512
ATTN_SLABS = 2
META_ROWS = 128
MASK_VALUE = -1e30
LOG2E = math.log2(math.e)

F32 = jnp.float32
BF16 = jnp.bfloat16


def _layer_norm(z, g, b):
    mu = jnp.mean(z, axis=-1, keepdims=True)
    zc = z - mu
    var = jnp.mean(zc * zc, axis=-1, keepdims=True)
    return zc * lax.rsqrt(var + LN_EPS) * g + b


def _const_spec(shape, single_buffer=False):
    zeros = (0,) * len(shape)
    if single_buffer:
        return pl.BlockSpec(shape, lambda *_: zeros, pipeline_mode=pl.Buffered(1))
    return pl.BlockSpec(shape, lambda *_: zeros)


def _swiglu(hb, wg_ref, wu_ref, wd_ref, fc):
    f_dim = wg_ref.shape[1]
    out = None
    for lo in range(0, f_dim, fc):
        sl = slice(lo, min(lo + fc, f_dim))
        hg = jnp.dot(hb, wg_ref[:, sl], preferred_element_type=F32)
        hu = jnp.dot(hb, wu_ref[:, sl], preferred_element_type=F32)
        a = (jax.nn.silu(hg) * hu).astype(BF16)
        part = jnp.dot(a, wd_ref[sl, :], preferred_element_type=F32)
        out = part if out is None else out + part
    return out


def _ffn_ln(h, wg_ref, wu_ref, wd_ref, g_ref, b_ref, fc):
    f = _swiglu(h.astype(BF16), wg_ref, wu_ref, wd_ref, fc)
    return _layer_norm(DN_ALPHA * h + f, g_ref[...], b_ref[...])


POOL_HALO = 2 * MAX_WINDOW
SUBLANES = 8


def _pool_mix(ext_ref, lvl_ref, row0, rows, t1, w_ref, sc_ref):
    d = ext_ref.shape[-1]
    cgrp = d // len(POOL_WINDOWS)
    zs = []
    for g, w in enumerate(POOL_WINDOWS):
        sl = slice(g * cgrp, (g + 1) * cgrp)
        arr = ext_ref[pl.ds(row0, rows + POOL_HALO), sl]
        cur_g = arr[POOL_HALO:]
        used = 0
        for k in (1, 2, 4):
            if k >= w:
                break
            n = arr.shape[0]
            lvl_ref[g, :n, :] = arr
            arr = arr[SUBLANES:] + lvl_ref[g, pl.ds(SUBLANES - k, n - SUBLANES), :]
            used += SUBLANES
        if w > SUBLANES:
            arr = arr[SUBLANES:] + arr[:-SUBLANES]
            used += SUBLANES
        win = arr[POOL_HALO - used:]
        y = win * (1.0 / jnp.minimum(t1, float(w))) - cur_g
        m = jnp.dot(y.astype(BF16), w_ref[g], preferred_element_type=F32) * sc_ref[:, sl]
        zs.append(DN_ALPHA * cur_g + m)
    return jnp.concatenate(zs, axis=1)


def _layer0_kernel(x_ref, halo_ref, meta_ref, pw_ref, psc_ref, g1_ref, b1_ref, wg_ref, wu_ref, wd_ref,
                   g2_ref, b2_ref, o_ref, ext_ref, lvl_ref, *, tm, sub, is_meta, fc):
    d = x_ref.shape[-1]
    ext_ref[POOL_HALO:, :] = x_ref[...]
    if is_meta:
        ext_ref[:POOL_HALO, :] = jnp.zeros((POOL_HALO, d), F32)
        pos0 = 0
    else:
        j = pl.program_id(1)

        @pl.when(j == 0)
        def _():
            ext_ref[:POOL_HALO - N_META, :] = jnp.zeros((POOL_HALO - N_META, d), F32)
            ext_ref[POOL_HALO - N_META:POOL_HALO, :] = meta_ref[...]

        @pl.when(j > 0)
        def _():
            ext_ref[:POOL_HALO, :] = halo_ref[...]

        pos0 = N_META + j * tm

    def mixer(r):
        t1 = (pos0 + r + 1 + lax.broadcasted_iota(jnp.int32, (sub, 1), 0)).astype(F32)
        return _layer_norm(_pool_mix(ext_ref, lvl_ref, r, sub, t1, pw_ref, psc_ref), g1_ref[...], b1_ref[...])

    h_next = mixer(0)
    for r in range(0, tm, sub):
        h1 = h_next
        if r + sub < tm:
            h_next = mixer(r + sub)
        o_ref[r:r + sub, :] = _ffn_ln(h1, wg_ref, wu_ref, wd_ref, g2_ref, b2_ref, fc)


def _layer1_tail_kernel(a_ref, h_ref, wo_ref, g1_ref, b1_ref, wg_ref, wu_ref, wd_ref, g2_ref, b2_ref,
                        o_ref, *, tm, sub, fc):
    def mixer(r):
        m = jnp.dot(a_ref[r:r + sub, :], wo_ref[...], preferred_element_type=F32)
        return _layer_norm(DN_ALPHA * h_ref[r:r + sub, :] + m, g1_ref[...], b1_ref[...])

    h_next = mixer(0)
    for r in range(0, tm, sub):
        h = h_next
        if r + sub < tm:
            h_next = mixer(r + sub)
        o_ref[r:r + sub, :] = _ffn_ln(h, wg_ref, wu_ref, wd_ref, g2_ref, b2_ref, fc)


def _ffn_vmem_limit(tm, sub, d, f, fc, extra):
    weights = 3 * d * f * 2
    tiles = 2 * 2 * tm * d * 4
    temps = 3 * (sub * fc * (4 + 4 + 4 + 2) + 4 * sub * d * 4)
    return min(weights + tiles + temps + extra + (6 << 20), VMEM_BUDGET)


def _layer_weight_spec(shape, layer):
    return pl.BlockSpec((None,) + shape, lambda *_: (layer, 0, 0), pipeline_mode=pl.Buffered(1))


def _layer0(x, meta_tokens, meta_pad, pw, psc, g1, b1, wg, wu, wd, g2, b2, *, layer, tm, sub, fc):
    bsz, s, d = x.shape
    f = wg.shape[-1]
    ngrp = len(POOL_WINDOWS)
    cgrp = d // ngrp
    hb = tm // POOL_HALO
    assert all(w & (w - 1) == 0 and w <= MAX_WINDOW == N_META for w in POOL_WINDOWS)
    vec = _const_spec((1, d))
    consts = [_const_spec((ngrp, cgrp, cgrp)), vec, vec, vec,
              _layer_weight_spec((d, f), layer), _layer_weight_spec((d, f), layer),
              _layer_weight_spec((f, d), layer), vec, vec]
    args = (pw, psc, g1, b1, wg, wu, wd, g2, b2)
    hx = pl.pallas_call(
        functools.partial(_layer0_kernel, tm=tm, sub=sub, is_meta=False, fc=fc),
        grid=(bsz, s // tm),
        in_specs=[pl.BlockSpec((None, tm, d), lambda bi, j: (bi, j, 0)),
                  pl.BlockSpec((None, POOL_HALO, d), lambda bi, j: (bi, jnp.maximum(j * hb - 1, 0), 0)),
                  _const_spec((N_META, d))] + consts,
        out_specs=pl.BlockSpec((None, tm, d), lambda bi, j: (bi, j, 0)),
        out_shape=jax.ShapeDtypeStruct((bsz, s, d), F32),
        scratch_shapes=[pltpu.VMEM((POOL_HALO + tm, d), F32), pltpu.VMEM((ngrp, POOL_HALO + sub, cgrp), F32)],
        compiler_params=pltpu.CompilerParams(
            dimension_semantics=("arbitrary", "arbitrary"),
            vmem_limit_bytes=_ffn_vmem_limit(tm, sub, d, f, fc, (2 * POOL_HALO + tm + sub) * d * 4)),
        name="layer0_x",
    )(x, x, meta_tokens, *args)
    hm = pl.pallas_call(
        functools.partial(_layer0_kernel, tm=META_ROWS, sub=META_ROWS, is_meta=True, fc=fc),
        grid=(1,),
        in_specs=[_const_spec((META_ROWS, d)), _const_spec((N_META, d)), _const_spec((N_META, d))] + consts,
        out_specs=_const_spec((META_ROWS, d)),
        out_shape=jax.ShapeDtypeStruct((META_ROWS, d), F32),
        scratch_shapes=[pltpu.VMEM((POOL_HALO + META_ROWS, d), F32),
                        pltpu.VMEM((ngrp, POOL_HALO + META_ROWS, cgrp), F32)],
        compiler_params=pltpu.CompilerParams(
            vmem_limit_bytes=_ffn_vmem_limit(META_ROWS, META_ROWS, d, f, fc, 0)),
        name="layer0_meta",
    )(meta_pad, meta_tokens, meta_tokens, *args)
    return hx, hm


def _layer1_tail(a, h, wo, g1, b1, wg, wu, wd, g2, b2, *, layer, tm, sub, fc):
    n, d = h.shape
    f = wg.shape[-1]
    row = pl.BlockSpec((tm, d), lambda i: (i, 0))
    vec = _const_spec((1, d))
    return pl.pallas_call(
        functools.partial(_layer1_tail_kernel, tm=tm, sub=sub, fc=fc),
        grid=(n // tm,),
        in_specs=[row, row, _const_spec((d, d), True), vec, vec,
                  _layer_weight_spec((d, f), layer), _layer_weight_spec((d, f), layer),
                  _layer_weight_spec((f, d), layer), vec, vec],
        out_specs=row,
        out_shape=jax.ShapeDtypeStruct((n, d), F32),
        compiler_params=pltpu.CompilerParams(
            dimension_semantics=("arbitrary",),
            vmem_limit_bytes=_ffn_vmem_limit(tm, sub, d, f, fc, d * d * 2 + 2 * tm * d * 2 + 2 * tm * d * 4)),
        name="layer1_tail",
    )(a, h, wo, g1, b1, wg, wu, wd, g2, b2)


BIAS_PIECES = 3
VT_ROWS = 80


def _bf16_split(x):
    hi = x.astype(BF16)
    r1 = x - hi.astype(F32)
    mid = r1.astype(BF16)
    lo = (r1 - mid.astype(F32)).astype(BF16)
    return hi, mid, lo


def _bias_constants(d):
    place = np.zeros((LANES, d), np.float32)
    lane_rows = np.zeros((8, d), np.float32)
    for h in range(N_HEADS):
        base = (h // 2) * LANES + (HEAD_DIM if h % 2 == 0 else 0)
        for piece in range(BIAS_PIECES):
            neg, pos = base + piece, base + BIAS_PIECES + piece
            place[piece * N_HEADS + h, [neg, pos]] = 1.0
            lane_rows[0, pos] = 1.0
            lane_rows[1, neg] = 1.0
            lane_rows[2, neg] = -1.0
            lane_rows[3, pos] = 1.0
    lane_rows[4] = (np.arange(d) % LANES) < HEAD_DIM
    return jnp.asarray(place, BF16), jnp.asarray(lane_rows, F32)


def _inproj_kernel(h_ref, wq_ref, wk_ref, wvf_ref, bf_ref, c0_ref, place_ref, lanes_ref,
                   qe_ref, qo_ref, ke_ref, ko_ref, vt_ref, c_ref, carry_ref, *, tm):
    d = h_ref.shape[-1]

    @pl.when(pl.program_id(1) == 0)
    def _():
        carry_ref[...] = c0_ref[...]

    hb = h_ref[...].astype(BF16)
    vf = lax.dot_general(wvf_ref[...], hb, (((1,), (1,)), ((), ())), preferred_element_type=F32)
    vt = vf[:d]
    extra = jnp.where(lax.broadcasted_iota(jnp.int32, (VT_ROWS - HEAD_DIM, tm), 0) == 0, 1.0, 0.0)
    rows = []
    for h in range(N_HEADS):
        rows += [vt[h * HEAD_DIM:(h + 1) * HEAD_DIM], extra]
    vt_ref[...] = jnp.concatenate(rows, axis=0).astype(BF16)

    z = vf[d:d + N_HEADS] + bf_ref[:, 0:1]
    logf = (jnp.minimum(z, 0.0) - jnp.log1p(jnp.exp(-jnp.abs(z)))) * LOG2E
    triu = (lax.broadcasted_iota(jnp.int32, (tm, tm), 0)
            <= lax.broadcasted_iota(jnp.int32, (tm, tm), 1)).astype(BF16)
    cum3 = jnp.dot(jnp.concatenate(_bf16_split(logf), axis=0), triu, preferred_element_type=F32)
    c = (carry_ref[:, 0:1] + cum3[:N_HEADS] + cum3[N_HEADS:2 * N_HEADS] + cum3[2 * N_HEADS:])
    c_ref[...] = c
    carry_ref[...] = jnp.broadcast_to(c[:, tm - 1:tm], (N_HEADS, LANES))

    q = jnp.dot(hb, wq_ref[...], preferred_element_type=F32) * (HEAD_DIM ** -0.5 * LOG2E)

    pieces = [p.astype(F32) for p in _bf16_split(c)]
    pad = jnp.zeros((LANES - BIAS_PIECES * N_HEADS, tm), F32)
    packed = jnp.concatenate(pieces + [pad], axis=0).T
    placed = jnp.dot(packed.astype(BF16), place_ref[...], preferred_element_type=F32)
    bias_q = placed * lanes_ref[0:1, :] + lanes_ref[1:2, :]
    bias_k = placed * lanes_ref[2:3, :] + lanes_ref[3:4, :]
    first = lanes_ref[4:5, :] > 0.5

    qe_ref[...] = jnp.where(first, q, bias_q).astype(BF16)
    qo_ref[...] = jnp.where(first, bias_q, q).astype(BF16)
    k = jnp.dot(hb, wk_ref[...], preferred_element_type=F32)
    ke_ref[...] = jnp.where(first, k, bias_k).astype(BF16)
    ko_ref[...] = jnp.where(first, bias_k, k).astype(BF16)


def _inproj(h, wq, wk, wvf, bfp, c0, place, lane_rows, *, tm):
    bsz, s, d = h.shape
    row = pl.BlockSpec((None, tm, d), lambda bi, j: (bi, j, 0))
    act = jax.ShapeDtypeStruct((bsz, s, d), BF16)
    wspec = _const_spec((d, d), True)
    stat = _const_spec((N_HEADS, LANES))
    return pl.pallas_call(
        functools.partial(_inproj_kernel, tm=tm),
        grid=(bsz, s // tm),
        in_specs=[row, wspec, wspec, _const_spec((d + N_HEADS, d), True), stat, stat,
                  _const_spec((LANES, d)), _const_spec((8, d))],
        out_specs=[row, row, row, row,
                   pl.BlockSpec((None, N_HEADS * VT_ROWS, tm), lambda bi, j: (bi, 0, j)),
                   pl.BlockSpec((None, N_HEADS, tm), lambda bi, j: (bi, 0, j))],
        out_shape=[act, act, act, act,
                   jax.ShapeDtypeStruct((bsz, N_HEADS * VT_ROWS, s), BF16),
                   jax.ShapeDtypeStruct((bsz, N_HEADS, s), F32)],
        scratch_shapes=[pltpu.VMEM((N_HEADS, LANES), F32)],
        compiler_params=pltpu.CompilerParams(dimension_semantics=("arbitrary", "arbitrary"),
                                             vmem_limit_bytes=STAGE_VMEM),
        name=f"inproj_{bsz * s}",
    )(h, wq, wk, wvf, bfp, c0, place, lane_rows)


def _attn_kernel(qe_ref, qo_ref, ke_ref, ko_ref, vt_ref, kme_ref, kmo_ref, vtm_ref, o_ref,
                 s_sc, mb_sc, p_sc, pm_sc, al_sc, m_sc, acc_sc, *, tq):
    nq = o_ref.shape[0] // tq
    nh = acc_sc.shape[0]

    def rows(i):
        return pl.ds(pl.multiple_of(i * tq, tq), tq)

    def slab(g):
        return slice((g // 2) * LANES, (g // 2 + 1) * LANES)

    def q_tile(g, jq):
        return (qe_ref, qo_ref)[g % 2][rows(jq), slab(g)]

    def k_chunk(g, kc):
        return (ke_ref, ko_ref)[g % 2][rows(kc), slab(g)]

    def k_meta(g):
        return (kme_ref, kmo_ref)[g % 2][:N_META, slab(g)]

    def vt_rows(ref, g, cols):
        return ref[g * VT_ROWS:(g + 1) * VT_ROWS, cols]

    def dot_nt(a, b):
        return lax.dot_general(a, b, (((1,), (1,)), ((), ())), preferred_element_type=F32)

    causal_t = (lax.broadcasted_iota(jnp.int32, (tq, tq), 0)
                <= lax.broadcasted_iota(jnp.int32, (tq, tq), 1))

    def put(ref, g, row):
        ref[g] = jnp.broadcast_to(row, (8, tq))

    def get(ref, g):
        return ref[g][0:1, :]

    def stage_a(jq, kc):
        for g in range(nh):
            st = dot_nt(k_chunk(g, kc), q_tile(g, jq))
            s_sc[g] = st
            put(mb_sc, g, jnp.max(st, axis=0, keepdims=True))

    def stage_b(jq, first, diag):
        for g in range(nh):
            st = s_sc[g]
            if diag:
                st = jnp.where(causal_t, st, MASK_VALUE)
                sm = dot_nt(k_meta(g), q_tile(g, jq))
                m_blk = jnp.maximum(jnp.max(st, axis=0, keepdims=True), jnp.max(sm, axis=0, keepdims=True))
            else:
                m_blk = get(mb_sc, g)
            if first:
                m_new = m_blk
                put(al_sc, g, jnp.zeros((1, tq), F32))
            else:
                m_old = get(m_sc, g)
                m_new = jnp.maximum(m_old, m_blk)
                put(al_sc, g, jnp.exp2(m_old - m_new))
            put(m_sc, g, m_new)
            p_sc[g] = jnp.exp2(st - m_new).astype(BF16)
            if diag:
                pm_sc[g, :N_META, :] = jnp.exp2(sm - m_new).astype(BF16)

    def stage_c(jq, kc, diag):
        outs = []
        for g in range(nh):
            pv = jnp.dot(vt_rows(vt_ref, g, rows(kc)), p_sc[g], preferred_element_type=F32)
            if diag:
                pv = pv + jnp.dot(vt_rows(vtm_ref, g, slice(None)), pm_sc[g], preferred_element_type=F32)
            acc = get(al_sc, g) * acc_sc[g] + pv
            acc_sc[g] = acc
            if diag:
                outs.append(acc[:HEAD_DIM] * (1.0 / acc[HEAD_DIM:HEAD_DIM + 1]))
        for g in range(0, len(outs), 2):
            o_ref[rows(jq), slab(g)] = jnp.concatenate(outs[g:g + 2], axis=0).T.astype(BF16)

    def inner_chunks(jq):
        def chunk(kc):
            stage_c(jq, kc - 1, False)
            stage_b(jq, False, False)
            stage_a(jq, kc + 1)

        odd = (jq - 1) % 2

        @pl.when(odd == 1)
        def _():
            chunk(1)

        def chunk_pair(i, c):
            kc = 1 + odd + 2 * i
            chunk(kc)
            chunk(kc + 1)
            return c

        lax.fori_loop(0, (jq - 1) // 2, chunk_pair, 0)

    acc_sc[...] = jnp.zeros(acc_sc.shape, F32)
    pm_sc[...] = jnp.zeros(pm_sc.shape, BF16)
    stage_a(0, 0)
    stage_b(0, True, True)
    stage_a(1, 0)

    def tile(jq, carry):
        stage_c(jq - 1, jq - 1, True)
        stage_b(jq, True, False)
        stage_a(jq, 1)
        inner_chunks(jq)
        stage_c(jq, jq - 1, False)
        stage_b(jq, False, True)
        stage_a(jnp.minimum(jq + 1, nq - 1), 0)
        return carry

    lax.fori_loop(1, nq, tile, 0)
    stage_c(nq - 1, nq - 1, True)


def _attention(qe, qo, ke, ko, vt, kme, kmo, vtm, *, tq, npair):
    bsz, s, d = ke.shape
    nh = 2 * npair
    width = npair * LANES
    seq = pl.BlockSpec((None, s, width), lambda bi, hp: (bi, 0, hp))
    kmspec = pl.BlockSpec((META_ROWS, width), lambda bi, hp: (0, hp))
    return pl.pallas_call(
        functools.partial(_attn_kernel, tq=tq),
        grid=(bsz, d // width),
        in_specs=[seq, seq, seq, seq,
                  pl.BlockSpec((None, nh * VT_ROWS, s), lambda bi, hp: (bi, hp, 0)),
                  kmspec, kmspec,
                  pl.BlockSpec((nh * VT_ROWS, META_ROWS), lambda bi, hp: (hp, 0))],
        out_specs=seq,
        out_shape=jax.ShapeDtypeStruct((bsz, s, d), BF16),
        scratch_shapes=[pltpu.VMEM((nh, tq, tq), F32), pltpu.VMEM((nh, 8, tq), F32),
                        pltpu.VMEM((nh, tq, tq), BF16), pltpu.VMEM((nh, META_ROWS, tq), BF16),
                        pltpu.VMEM((nh, 8, tq), F32), pltpu.VMEM((nh, 8, tq), F32),
                        pltpu.VMEM((nh, VT_ROWS, tq), F32)],
        compiler_params=pltpu.CompilerParams(dimension_semantics=("arbitrary", "arbitrary"),
                                             vmem_limit_bytes=STAGE_VMEM),
        name="fox_attn",
    )(qe, qo, ke, ko, vt, kme, kmo, vtm)


def kernel(x, meta_tokens, pool_w, pool_scale, fox_w_in, fox_b_f, fox_w_o, ffn_w_gate, ffn_w_up,
           ffn_w_down, ln_g, ln_b):
    bsz, s, d = x.shape
    f_dim = ffn_w_gate.shape[-1]
    tm, sub = FFN_ROWS, FFN_SUB_ROWS
    fc = -(-f_dim // (2 * MXU_TILE)) * MXU_TILE
    row = lambda a: a.reshape(1, -1).astype(F32)

    meta_pad = jnp.pad(meta_tokens, ((0, META_ROWS - N_META), (0, 0)))
    wg = ffn_w_gate.astype(BF16)
    wu = ffn_w_up.astype(BF16)
    wd = ffn_w_down.astype(BF16)

    h2x, h2m = _layer0(x, meta_tokens, meta_pad, pool_w[0].astype(BF16), row(pool_scale[0]),
                       row(ln_g[0, 0]), row(ln_b[0, 0]), wg, wu, wd, row(ln_g[0, 1]), row(ln_b[0, 1]),
                       layer=0, tm=tm, sub=sub, fc=fc)
    h2x = h2x.reshape(bsz * s, d)

    w_in = fox_w_in[0]
    wq, wk = (w_in[:, i * d:(i + 1) * d].astype(BF16) for i in range(2))
    wvf = w_in[:, 2 * d:].T.astype(BF16)
    bfp = jnp.broadcast_to(fox_b_f[0][:, None], (N_HEADS, LANES))
    place, lane_rows = _bias_constants(d)
    inproj = functools.partial(_inproj, wq=wq, wk=wk, wvf=wvf, bfp=bfp, place=place, lane_rows=lane_rows)
    _, _, kme, kmo, vtm, cm = inproj(h2m[None], c0=jnp.zeros((N_HEADS, LANES), F32), tm=META_ROWS)
    c0 = jnp.broadcast_to(cm[0, :, N_META - 1:N_META], (N_HEADS, LANES))
    qe, qo, ke, ko, vt, _ = inproj(h2x.reshape(bsz, s, d), c0=c0, tm=INPROJ_ROWS)
    o = _attention(qe, qo, ke, ko, vt, kme[0], kmo[0], vtm[0], tq=ATTN_TILE, npair=ATTN_SLABS)
    out = _layer1_tail(o.reshape(bsz * s, d), h2x, fox_w_o[0].astype(BF16), row(ln_g[1, 0]), row(ln_b[1, 0]),
                       wg, wu, wd, row(ln_g[1, 1]), row(ln_b[1, 1]), layer=1, tm=tm, sub=sub, fc=fc)
    return out.reshape(bsz, s, d)
```

```python
import functools
import math

import jax
import jax.numpy as jnp
import numpy as np
from jax import lax
from jax.experimental import pallas as pl
from jax.experimental.pallas import tpu as pltpu

N_META = 16
N_HEADS = 16
HEAD_DIM = 64
POOL_WINDOWS = (2, 4, 8, 16)
MAX_WINDOW = max(POOL_WINDOWS)
DEPTH = 2
DN_ALPHA = (2.0 * DEPTH) ** 0.25
LN_EPS = 1e-5
LANES = 128
MXU_TILE = 256
VMEM_BUDGET = 60 << 20
STAGE_VMEM = 56 << 20

FFN_ROWS = 1024
FFN_SUB_ROWS = 256
INPROJ_ROWS = 1024
ATTN_TILE = 512
ATTN_SLABS = 2
META_ROWS = 128
MASK_VALUE = -1e30
LOG2E = math.log2(math.e)

F32 = jnp.float32
BF16 = jnp.bfloat16


def _layer_norm(z, g, b):
    mu = jnp.mean(z, axis=-1, keepdims=True)
    zc = z - mu
    var = jnp.mean(zc * zc, axis=-1, keepdims=True)
    return zc * lax.rsqrt(var + LN_EPS) * g + b


def _const_spec(shape, single_buffer=False):
    zeros = (0,) * len(shape)
    if single_buffer:
        return pl.BlockSpec(shape, lambda *_: zeros, pipeline_mode=pl.Buffered(1))
    return pl.BlockSpec(shape, lambda *_: zeros)


def _swiglu(hb, wg_ref, wu_ref, wd_ref, fc):
    f_dim = wg_ref.shape[1]
    out = None
    for lo in range(0, f_dim, fc):
        sl = slice(lo, min(lo + fc, f_dim))
        hg = jnp.dot(hb, wg_ref[:, sl], preferred_element_type=F32)
        hu = jnp.dot(hb, wu_ref[:, sl], preferred_element_type=F32)
        a = (jax.nn.silu(hg) * hu).astype(BF16)
        part = jnp.dot(a, wd_ref[sl, :], preferred_element_type=F32)
        out = part if out is None else out + part
    return out


def _ffn_ln(h, wg_ref, wu_ref, wd_ref, g_ref, b_ref, fc):
    f = _swiglu(h.astype(BF16), wg_ref, wu_ref, wd_ref, fc)
    return _layer_norm(DN_ALPHA * h + f, g_ref[...], b_ref[...])


POOL_HALO = 2 * MAX_WINDOW
SUBLANES = 8


def _pool_mix(ext_ref, lvl_ref, row0, rows, t1, w_ref, sc_ref):
    d = ext_ref.shape[-1]
    cgrp = d // len(POOL_WINDOWS)
    zs = []
    for g, w in enumerate(POOL_WINDOWS):
        sl = slice(g * cgrp, (g + 1) * cgrp)
        arr = ext_ref[pl.ds(row0, rows + POOL_HALO), sl]
        cur_g = arr[POOL_HALO:]
        used = 0
        for k in (1, 2, 4):
            if k >= w:
                break
            n = arr.shape[0]
            lvl_ref[g, :n, :] = arr
            arr = arr[SUBLANES:] + lvl_ref[g, pl.ds(SUBLANES - k, n - SUBLANES), :]
            used += SUBLANES
        if w > SUBLANES:
            arr = arr[SUBLANES:] + arr[:-SUBLANES]
            used += SUBLANES
        win = arr[POOL_HALO - used:]
        y = win / jnp.minimum(t1, float(w)) - cur_g
        m = jnp.dot(y.astype(BF16), w_ref[g], preferred_element_type=F32) * sc_ref[:, sl]
        zs.append(DN_ALPHA * cur_g + m)
    return jnp.concatenate(zs, axis=1)


def _layer0_kernel(x_ref, halo_ref, meta_ref, pw_ref, psc_ref, g1_ref, b1_ref, wg_ref, wu_ref, wd_ref,
                   g2_ref, b2_ref, o_ref, ext_ref, lvl_ref, *, tm, sub, is_meta, fc):
    d = x_ref.shape[-1]
    ext_ref[POOL_HALO:, :] = x_ref[...]
    if is_meta:
        ext_ref[:POOL_HALO, :] = jnp.zeros((POOL_HALO, d), F32)
        pos0 = 0
    else:
        j = pl.program_id(1)

        @pl.when(j == 0)
        def _():
            ext_ref[:POOL_HALO - N_META, :] = jnp.zeros((POOL_HALO - N_META, d), F32)
            ext_ref[POOL_HALO - N_META:POOL_HALO, :] = meta_ref[...]

        @pl.when(j > 0)
        def _():
            ext_ref[:POOL_HALO, :] = halo_ref[...]

        pos0 = N_META + j * tm

    def mixer(r):
        t1 = (pos0 + r + 1 + lax.broadcasted_iota(jnp.int32, (sub, 1), 0)).astype(F32)
        return _layer_norm(_pool_mix(ext_ref, lvl_ref, r, sub, t1, pw_ref, psc_ref), g1_ref[...], b1_ref[...])

    h_next = mixer(0)
    for r in range(0, tm, sub):
        h1 = h_next
        if r + sub < tm:
            h_next = mixer(r + sub)
        o_ref[r:r + sub, :] = _ffn_ln(h1, wg_ref, wu_ref, wd_ref, g2_ref, b2_ref, fc)


def _layer1_tail_kernel(a_ref, h_ref, wo_ref, g1_ref, b1_ref, wg_ref, wu_ref, wd_ref, g2_ref, b2_ref,
                        o_ref, *, tm, sub, fc):
    def mixer(r):
        m = jnp.dot(a_ref[r:r + sub, :], wo_ref[...], preferred_element_type=F32)
        return _layer_norm(DN_ALPHA * h_ref[r:r + sub, :] + m, g1_ref[...], b1_ref[...])

    h_next = mixer(0)
    for r in range(0, tm, sub):
        h = h_next
        if r + sub < tm:
            h_next = mixer(r + sub)
        o_ref[r:r + sub, :] = _ffn_ln(h, wg_ref, wu_ref, wd_ref, g2_ref, b2_ref, fc)


def _ffn_vmem_limit(tm, sub, d, f, fc, extra):
    weights = 3 * d * f * 2
    tiles = 2 * 2 * tm * d * 4
    temps = 3 * (sub * fc * (4 + 4 + 4 + 2) + 4 * sub * d * 4)
    return min(weights + tiles + temps + extra + (6 << 20), VMEM_BUDGET)


def _layer_weight_spec(shape, layer):
    return pl.BlockSpec((None,) + shape, lambda *_: (layer, 0, 0), pipeline_mode=pl.Buffered(1))


def _layer0(x, meta_tokens, meta_pad, pw, psc, g1, b1, wg, wu, wd, g2, b2, *, layer, tm, sub, fc):
    bsz, s, d = x.shape
    f = wg.shape[-1]
    ngrp = len(POOL_WINDOWS)
    cgrp = d // ngrp
    hb = tm // POOL_HALO
    assert all(w & (w - 1) == 0 and w <= MAX_WINDOW == N_META for w in POOL_WINDOWS)
    vec = _const_spec((1, d))
    consts = [_const_spec((ngrp, cgrp, cgrp)), vec, vec, vec,
              _layer_weight_spec((d, f), layer), _layer_weight_spec((d, f), layer),
              _layer_weight_spec((f, d), layer), vec, vec]
    args = (pw, psc, g1, b1, wg, wu, wd, g2, b2)
    hx = pl.pallas_call(
        functools.partial(_layer0_kernel, tm=tm, sub=sub, is_meta=False, fc=fc),
        grid=(bsz, s // tm),
        in_specs=[pl.BlockSpec((None, tm, d), lambda bi, j: (bi, j, 0)),
                  pl.BlockSpec((None, POOL_HALO, d), lambda bi, j: (bi, jnp.maximum(j * hb - 1, 0), 0)),
                  _const_spec((N_META, d))] + consts,
        out_specs=pl.BlockSpec((None, tm, d), lambda bi, j: (bi, j, 0)),
        out_shape=jax.ShapeDtypeStruct((bsz, s, d), F32),
        scratch_shapes=[pltpu.VMEM((POOL_HALO + tm, d), F32), pltpu.VMEM((ngrp, POOL_HALO + sub, cgrp), F32)],
        compiler_params=pltpu.CompilerParams(
            dimension_semantics=("arbitrary", "arbitrary"),
            allow_input_fusion=[False] * 7 + [True] * 3 + [False] * 2,
            vmem_limit_bytes=_ffn_vmem_limit(tm, sub, d, f, fc, (2 * POOL_HALO + tm + sub) * d * 4)),
        name="layer0_x",
    )(x, x, meta_tokens, *args)
    hm = pl.pallas_call(
        functools.partial(_layer0_kernel, tm=META_ROWS, sub=META_ROWS, is_meta=True, fc=fc),
        grid=(1,),
        in_specs=[_const_spec((META_ROWS, d)), _const_spec((N_META, d)), _const_spec((N_META, d))] + consts,
        out_specs=_const_spec((META_ROWS, d)),
        out_shape=jax.ShapeDtypeStruct((META_ROWS, d), F32),
        scratch_shapes=[pltpu.VMEM((POOL_HALO + META_ROWS, d), F32),
                        pltpu.VMEM((ngrp, POOL_HALO + META_ROWS, cgrp), F32)],
        compiler_params=pltpu.CompilerParams(
            vmem_limit_bytes=_ffn_vmem_limit(META_ROWS, META_ROWS, d, f, fc, 0)),
        name="layer0_meta",
    )(meta_pad, meta_tokens, meta_tokens, *args)
    return hx, hm


def _layer1_tail(a, h, wo, g1, b1, wg, wu, wd, g2, b2, *, layer, tm, sub, fc):
    n, d = h.shape
    f = wg.shape[-1]
    row = pl.BlockSpec((tm, d), lambda i: (i, 0))
    vec = _const_spec((1, d))
    return pl.pallas_call(
        functools.partial(_layer1_tail_kernel, tm=tm, sub=sub, fc=fc),
        grid=(n // tm,),
        in_specs=[row, row, _const_spec((d, d), True), vec, vec,
                  _layer_weight_spec((d, f), layer), _layer_weight_spec((d, f), layer),
                  _layer_weight_spec((f, d), layer), vec, vec],
        out_specs=row,
        out_shape=jax.ShapeDtypeStruct((n, d), F32),
        compiler_params=pltpu.CompilerParams(
            dimension_semantics=("arbitrary",),
            allow_input_fusion=[False] * 5 + [True] * 3 + [False] * 2,
            vmem_limit_bytes=_ffn_vmem_limit(tm, sub, d, f, fc, d * d * 2 + 2 * tm * d * 2 + 2 * tm * d * 4)),
        name="layer1_tail",
    )(a, h, wo, g1, b1, wg, wu, wd, g2, b2)


BIAS_PIECES = 3
VT_ROWS = 80


def _bf16_split(x):
    hi = x.astype(BF16)
    r1 = x - hi.astype(F32)
    mid = r1.astype(BF16)
    lo = (r1 - mid.astype(F32)).astype(BF16)
    return hi, mid, lo


def _bias_constants(d):
    place = np.zeros((LANES, d), np.float32)
    lane_rows = np.zeros((8, d), np.float32)
    for h in range(N_HEADS):
        base = (h // 2) * LANES + (HEAD_DIM if h % 2 == 0 else 0)
        for piece in range(BIAS_PIECES):
            neg, pos = base + piece, base + BIAS_PIECES + piece
            place[piece * N_HEADS + h, [neg, pos]] = 1.0
            lane_rows[0, pos] = 1.0
            lane_rows[1, neg] = 1.0
            lane_rows[2, neg] = -1.0
            lane_rows[3, pos] = 1.0
    lane_rows[4] = (np.arange(d) % LANES) < HEAD_DIM
    return jnp.asarray(place, BF16), jnp.asarray(lane_rows, F32)


def _inproj_kernel(h_ref, wq_ref, wk_ref, wvf_ref, bf_ref, c0_ref, place_ref, lanes_ref,
                   qe_ref, qo_ref, ke_ref, ko_ref, vt_ref, c_ref, carry_ref, *, tm):
    d = h_ref.shape[-1]

    @pl.when(pl.program_id(1) == 0)
    def _():
        carry_ref[...] = c0_ref[...]

    hb = h_ref[...].astype(BF16)
    vf = lax.dot_general(wvf_ref[...], hb, (((1,), (1,)), ((), ())), preferred_element_type=F32)
    vt = vf[:d]
    extra = jnp.where(lax.broadcasted_iota(jnp.int32, (VT_ROWS - HEAD_DIM, tm), 0) == 0, 1.0, 0.0)
    rows = []
    for h in range(N_HEADS):
        rows += [vt[h * HEAD_DIM:(h + 1) * HEAD_DIM], extra]
    vt_ref[...] = jnp.concatenate(rows, axis=0).astype(BF16)

    z = vf[d:d + N_HEADS] + bf_ref[:, 0:1]
    logf = (jnp.minimum(z, 0.0) - jnp.log1p(jnp.exp(-jnp.abs(z)))) * LOG2E
    triu = (lax.broadcasted_iota(jnp.int32, (tm, tm), 0)
            <= lax.broadcasted_iota(jnp.int32, (tm, tm), 1)).astype(BF16)
    cum3 = jnp.dot(jnp.concatenate(_bf16_split(logf), axis=0), triu, preferred_element_type=F32)
    c = (carry_ref[:, 0:1] + cum3[:N_HEADS] + cum3[N_HEADS:2 * N_HEADS] + cum3[2 * N_HEADS:])
    c_ref[...] = c
    carry_ref[...] = jnp.broadcast_to(c[:, tm - 1:tm], (N_HEADS, LANES))

    q = jnp.dot(hb, wq_ref[...], preferred_element_type=F32) * (HEAD_DIM ** -0.5 * LOG2E)

    pieces = [p.astype(F32) for p in _bf16_split(c)]
    pad = jnp.zeros((LANES - BIAS_PIECES * N_HEADS, tm), F32)
    packed = jnp.concatenate(pieces + [pad], axis=0).T
    placed = jnp.dot(packed.astype(BF16), place_ref[...], preferred_element_type=F32)
    bias_q = placed * lanes_ref[0:1, :] + lanes_ref[1:2, :]
    bias_k = placed * lanes_ref[2:3, :] + lanes_ref[3:4, :]
    first = lanes_ref[4:5, :] > 0.5

    qe_ref[...] = jnp.where(first, q, bias_q).astype(BF16)
    qo_ref[...] = jnp.where(first, bias_q, q).astype(BF16)
    k = jnp.dot(hb, wk_ref[...], preferred_element_type=F32)
    ke_ref[...] = jnp.where(first, k, bias_k).astype(BF16)
    ko_ref[...] = jnp.where(first, bias_k, k).astype(BF16)


def _inproj(h, wq, wk, wvf, bfp, c0, place, lane_rows, *, tm):
    bsz, s, d = h.shape
    row = pl.BlockSpec((None, tm, d), lambda bi, j: (bi, j, 0))
    act = jax.ShapeDtypeStruct((bsz, s, d), BF16)
    wspec = _const_spec((d, d), True)
    stat = _const_spec((N_HEADS, LANES))
    return pl.pallas_call(
        functools.partial(_inproj_kernel, tm=tm),
        grid=(bsz, s // tm),
        in_specs=[row, wspec, wspec, _const_spec((d + N_HEADS, d), True), stat, stat,
                  _const_spec((LANES, d)), _const_spec((8, d))],
        out_specs=[row, row, row, row,
                   pl.BlockSpec((None, N_HEADS * VT_ROWS, tm), lambda bi, j: (bi, 0, j)),
                   pl.BlockSpec((None, N_HEADS, tm), lambda bi, j: (bi, 0, j))],
        out_shape=[act, act, act, act,
                   jax.ShapeDtypeStruct((bsz, N_HEADS * VT_ROWS, s), BF16),
                   jax.ShapeDtypeStruct((bsz, N_HEADS, s), F32)],
        scratch_shapes=[pltpu.VMEM((N_HEADS, LANES), F32)],
        compiler_params=pltpu.CompilerParams(dimension_semantics=("arbitrary", "arbitrary"),
                                             vmem_limit_bytes=STAGE_VMEM),
        name=f"inproj_{bsz * s}",
    )(h, wq, wk, wvf, bfp, c0, place, lane_rows)


def _attn_kernel(qe_ref, qo_ref, ke_ref, ko_ref, vt_ref, kme_ref, kmo_ref, vtm_ref, o_ref,
                 s_sc, mb_sc, p_sc, pm_sc, al_sc, m_sc, acc_sc, *, tq):
    nq = o_ref.shape[0] // tq
    nh = acc_sc.shape[0]

    def rows(i):
        return pl.ds(pl.multiple_of(i * tq, tq), tq)

    def slab(g):
        return slice((g // 2) * LANES, (g // 2 + 1) * LANES)

    def q_tile(g, jq):
        return (qe_ref, qo_ref)[g % 2][rows(jq), slab(g)]

    def k_chunk(g, kc):
        return (ke_ref, ko_ref)[g % 2][rows(kc), slab(g)]

    def k_meta(g):
        return (kme_ref, kmo_ref)[g % 2][:N_META, slab(g)]

    def vt_rows(ref, g, cols):
        return ref[g * VT_ROWS:(g + 1) * VT_ROWS, cols]

    def dot_nt(a, b):
        return lax.dot_general(a, b, (((1,), (1,)), ((), ())), preferred_element_type=F32)

    causal_t = (lax.broadcasted_iota(jnp.int32, (tq, tq), 0)
                <= lax.broadcasted_iota(jnp.int32, (tq, tq), 1))

    def put(ref, g, row):
        ref[g] = jnp.broadcast_to(row, (8, tq))

    def get(ref, g):
        return ref[g][0:1, :]

    def stage_a(jq, kc):
        for g in range(nh):
            st = dot_nt(k_chunk(g, kc), q_tile(g, jq))
            s_sc[g] = st
            put(mb_sc, g, jnp.max(st, axis=0, keepdims=True))

    def stage_b(jq, first, diag):
        for g in range(nh):
            st = s_sc[g]
            if diag:
                st = jnp.where(causal_t, st, MASK_VALUE)
                sm = dot_nt(k_meta(g), q_tile(g, jq))
                m_blk = jnp.maximum(jnp.max(st, axis=0, keepdims=True), jnp.max(sm, axis=0, keepdims=True))
            else:
                m_blk = get(mb_sc, g)
            if first:
                m_new = m_blk
                put(al_sc, g, jnp.zeros((1, tq), F32))
            else:
                m_old = get(m_sc, g)
                m_new = jnp.maximum(m_old, m_blk)
                put(al_sc, g, jnp.exp2(m_old - m_new))
            put(m_sc, g, m_new)
            p_sc[g] = jnp.exp2(st - m_new).astype(BF16)
            if diag:
                pm_sc[g, :N_META, :] = jnp.exp2(sm - m_new).astype(BF16)

    def stage_c(jq, kc, diag):
        outs = []
        for g in range(nh):
            pv = jnp.dot(vt_rows(vt_ref, g, rows(kc)), p_sc[g], preferred_element_type=F32)
            if diag:
                pv = pv + jnp.dot(vt_rows(vtm_ref, g, slice(None)), pm_sc[g], preferred_element_type=F32)
            acc = get(al_sc, g) * acc_sc[g] + pv
            acc_sc[g] = acc
            if diag:
                outs.append(acc[:HEAD_DIM] / acc[HEAD_DIM:HEAD_DIM + 1])
        for g in range(0, len(outs), 2):
            o_ref[rows(jq), slab(g)] = jnp.concatenate(outs[g:g + 2], axis=0).T.astype(BF16)

    def inner_chunks(jq):
        def chunk(kc):
            stage_c(jq, kc - 1, False)
            stage_b(jq, False, False)
            stage_a(jq, kc + 1)

        odd = (jq - 1) % 2

        @pl.when(odd == 1)
        def _():
            chunk(1)

        def chunk_pair(i, c):
            kc = 1 + odd + 2 * i
            chunk(kc)
            chunk(kc + 1)
            return c

        lax.fori_loop(0, (jq - 1) // 2, chunk_pair, 0)

    acc_sc[...] = jnp.zeros(acc_sc.shape, F32)
    pm_sc[...] = jnp.zeros(pm_sc.shape, BF16)
    stage_a(0, 0)
    stage_b(0, True, True)
    stage_a(1, 0)

    def tile(jq, carry):
        stage_c(jq - 1, jq - 1, True)
        stage_b(jq, True, False)
        stage_a(jq, 1)
        inner_chunks(jq)
        stage_c(jq, jq - 1, False)
        stage_b(jq, False, True)
        stage_a(jnp.minimum(jq + 1, nq - 1), 0)
        return carry

    lax.fori_loop(1, nq, tile, 0)
    stage_c(nq - 1, nq - 1, True)


def _attention(qe, qo, ke, ko, vt, kme, kmo, vtm, *, tq, npair):
    bsz, s, d = ke.shape
    nh = 2 * npair
    width = npair * LANES
    seq = pl.BlockSpec((None, s, width), lambda bi, hp: (bi, 0, hp))
    kmspec = pl.BlockSpec((META_ROWS, width), lambda bi, hp: (0, hp))
    return pl.pallas_call(
        functools.partial(_attn_kernel, tq=tq),
        grid=(bsz, d // width),
        in_specs=[seq, seq, seq, seq,
                  pl.BlockSpec((None, nh * VT_ROWS, s), lambda bi, hp: (bi, hp, 0)),
                  kmspec, kmspec,
                  pl.BlockSpec((nh * VT_ROWS, META_ROWS), lambda bi, hp: (hp, 0))],
        out_specs=seq,
        out_shape=jax.ShapeDtypeStruct((bsz, s, d), BF16),
        scratch_shapes=[pltpu.VMEM((nh, tq, tq), F32), pltpu.VMEM((nh, 8, tq), F32),
                        pltpu.VMEM((nh, tq, tq), BF16), pltpu.VMEM((nh, META_ROWS, tq), BF16),
                        pltpu.VMEM((nh, 8, tq), F32), pltpu.VMEM((nh, 8, tq), F32),
                        pltpu.VMEM((nh, VT_ROWS, tq), F32)],
        compiler_params=pltpu.CompilerParams(dimension_semantics=("arbitrary", "arbitrary"),
                                             vmem_limit_bytes=STAGE_VMEM),
        name="fox_attn",
    )(qe, qo, ke, ko, vt, kme, kmo, vtm)


def kernel(x, meta_tokens, pool_w, pool_scale, fox_w_in, fox_b_f, fox_w_o, ffn_w_gate, ffn_w_up,
           ffn_w_down, ln_g, ln_b):
    bsz, s, d = x.shape
    f_dim = ffn_w_gate.shape[-1]
    tm, sub = FFN_ROWS, FFN_SUB_ROWS
    fc = -(-f_dim // (2 * MXU_TILE)) * MXU_TILE
    row = lambda a: a.reshape(1, -1).astype(F32)

    meta_pad = jnp.pad(meta_tokens, ((0, META_ROWS - N_META), (0, 0)))
    wg = ffn_w_gate.astype(BF16)
    wu = ffn_w_up.astype(BF16)
    wd = ffn_w_down.astype(BF16)

    h2x, h2m = _layer0(x, meta_tokens, meta_pad, pool_w[0].astype(BF16), row(pool_scale[0]),
                       row(ln_g[0, 0]), row(ln_b[0, 0]), wg, wu, wd, row(ln_g[0, 1]), row(ln_b[0, 1]),
                       layer=0, tm=tm, sub=sub, fc=fc)
    h2x = h2x.reshape(bsz * s, d)

    w_in = fox_w_in[0]
    wq, wk = (w_in[:, i * d:(i + 1) * d].astype(BF16) for i in range(2))
    wvf = w_in[:, 2 * d:].T.astype(BF16)
    bfp = jnp.broadcast_to(fox_b_f[0][:, None], (N_HEADS, LANES))
    place, lane_rows = _bias_constants(d)
    inproj = functools.partial(_inproj, wq=wq, wk=wk, wvf=wvf, bfp=bfp, place=place, lane_rows=lane_rows)
    _, _, kme, kmo, vtm, cm = inproj(h2m[None], c0=jnp.zeros((N_HEADS, LANES), F32), tm=META_ROWS)
    c0 = jnp.broadcast_to(cm[0, :, N_META - 1:N_META], (N_HEADS, LANES))
    qe, qo, ke, ko, vt, _ = inproj(h2x.reshape(bsz, s, d), c0=c0, tm=INPROJ_ROWS)
    o = _attention(qe, qo, ke, ko, vt, kme[0], kmo[0], vtm[0], tq=ATTN_TILE, npair=ATTN_SLABS)
    out = _layer1_tail(o.reshape(bsz * s, d), h2x, fox_w_o[0].astype(BF16), row(ln_g[1, 0]), row(ln_b[1, 0]),
                       wg, wu, wd, row(ln_g[1, 1]), row(ln_b[1, 1]), layer=1, tm=tm, sub=sub, fc=fc)
    return out.reshape(bsz, s, d)
```
